```python
import math, functools
import jax, jax.numpy as jnp
from jax import lax
import numpy as np

D_MODEL = 2048
BATCH = 4
SEQ = 2048
DEPTH = 4
DEC_BATCH = 8
DEC_SEQ = 4
PAST_LEN = 16384
PAGE_SIZE = 128

N_MIXERS = 2
N_A_LAYERS = (DEPTH + 1) // 2
N_B_LAYERS = DEPTH // 2
D_FF = 5632
EPS = 1e-6

GDN_QK_HEADS = 16
GDN_V_HEADS = 32
GDN_HEAD_K = 128
GDN_HEAD_V = 128
GDN_CONV = 4
GDN_CHUNK = 64
GDN_KEY_DIM = GDN_QK_HEADS * GDN_HEAD_K
GDN_VAL_DIM = GDN_V_HEADS * GDN_HEAD_V
GDN_CONV_DIM = 2 * GDN_KEY_DIM + GDN_VAL_DIM
GDN_IN_DIM = GDN_CONV_DIM + GDN_VAL_DIM + 2 * GDN_V_HEADS

DIFF_HEADS = 8
DIFF_HEAD = 128
DIFF_V_HEAD = 2 * DIFF_HEAD
DIFF_QK_DIM = DIFF_HEADS * 2 * DIFF_HEAD
DIFF_V_DIM = DIFF_HEADS * DIFF_V_HEAD
DIFF_IN_DIM = 2 * DIFF_QK_DIM + DIFF_V_DIM
ROPE_DIM = DIFF_HEAD // 4
ROPE_THETA = 500000.0
Q_BLOCK = 128

kernel_name = "hybrid_gdn_diffattn_macaron_step"


def rmsnorm(x, g):
    xf = x.astype(jnp.float32)
    y = xf * lax.rsqrt(jnp.mean(xf * xf, axis=-1, keepdims=True) + EPS)
    return (y * g.astype(jnp.float32)).astype(x.dtype)


def half_ffn(x, g, w_gu, w_dn):
    h = rmsnorm(x, g) @ w_gu
    gate, up = jnp.split(h, 2, axis=-1)
    return x + 0.5 * ((jax.nn.silu(gate) * up) @ w_dn)


def l2norm(x):
    xf = x.astype(jnp.float32)
    return xf * lax.rsqrt(jnp.sum(xf * xf, axis=-1, keepdims=True) + EPS)


def causal_conv_silu(x, prev, w):
    T = x.shape[1]
    xp = jnp.concatenate([prev.astype(x.dtype), x], axis=1)
    y = xp[:, 0:T] * w[0]
    for j in range(1, GDN_CONV):
        y = y + xp[:, j:j + T] * w[j]
    return jax.nn.silu(y), xp[:, T:]


def _to_chunks(x, n):
    B, T, H = x.shape[:3]
    x = x.reshape((B, n, GDN_CHUNK, H) + x.shape[3:])
    return x.transpose((0, 3, 1, 2) + tuple(range(4, x.ndim)))


def gated_delta_rule(q, k, v, g, beta, s0):
    B, T, H, DK = k.shape
    pad = (-T) % GDN_CHUNK
    n = (T + pad) // GDN_CHUNK

    def prep(t):
        t = jnp.pad(t, [(0, 0), (0, pad)] + [(0, 0)] * (t.ndim - 2))
        return _to_chunks(t, n)

    q, k, v, g, beta = (prep(t) for t in (q * DK ** -0.5, k, v, g, beta))
    gc = jnp.cumsum(g, axis=-1)
    lower = jnp.tril(jnp.ones((GDN_CHUNK, GDN_CHUNK), bool))
    strict = jnp.tril(jnp.ones((GDN_CHUNK, GDN_CHUNK), bool), -1)
    gdiff = gc[..., :, None] - gc[..., None, :]
    decay = jnp.where(lower, jnp.exp(jnp.where(lower, gdiff, 0.0)), 0.0)
    k_beta = k * beta[..., None]
    m = jnp.where(strict, jnp.einsum("bhnid,bhnjd->bhnij", k_beta, k) * decay, 0.0)
    solve = functools.partial(lax.linalg.triangular_solve, left_side=True, lower=True, unit_diagonal=True)
    u = solve(m, v * beta[..., None])
    w = solve(m, k_beta * jnp.exp(gc)[..., None])
    a_intra = jnp.einsum("bhnid,bhnjd->bhnij", q, k) * decay

    def step(S, xs):
        q_c, k_c, u_c, w_c, a_c, g_c = xs
        v_new = u_c - jnp.einsum("bhcd,bhde->bhce", w_c, S)
        o = (jnp.einsum("bhcd,bhde->bhce", q_c * jnp.exp(g_c)[..., None], S)
             + jnp.einsum("bhij,bhje->bhie", a_c, v_new))
        g_last = g_c[..., -1:]
        S = (S * jnp.exp(g_last)[..., None]
             + jnp.einsum("bhcd,bhce->bhde", k_c * jnp.exp(g_last - g_c)[..., None], v_new))
        return S, o

    xs = tuple(jnp.moveaxis(t, 2, 0) for t in (q, k, u, w, a_intra, gc))
    S, o = lax.scan(step, s0, xs)
    o = o.transpose(1, 0, 3, 2, 4).reshape(B, n * GDN_CHUNK, H, -1)[:, :T]
    return o, S


def gdn_mixer(h, conv_prev, s0, w_in, conv_w, a_log, dt_bias, norm_w, w_out):
    B, T, _ = h.shape
    p = h @ w_in
    qkv = p[..., :GDN_CONV_DIM]
    z = p[..., GDN_CONV_DIM:GDN_CONV_DIM + GDN_VAL_DIM]
    b = p[..., GDN_CONV_DIM + GDN_VAL_DIM:GDN_CONV_DIM + GDN_VAL_DIM + GDN_V_HEADS]
    a = p[..., GDN_CONV_DIM + GDN_VAL_DIM + GDN_V_HEADS:]
    qkv, conv_state = causal_conv_silu(qkv, conv_prev, conv_w)
    rep = GDN_V_HEADS // GDN_QK_HEADS
    q = jnp.repeat(l2norm(qkv[..., :GDN_KEY_DIM].reshape(B, T, GDN_QK_HEADS, GDN_HEAD_K)), rep, axis=2)
    k = jnp.repeat(l2norm(qkv[..., GDN_KEY_DIM:2 * GDN_KEY_DIM].reshape(B, T, GDN_QK_HEADS, GDN_HEAD_K)), rep, axis=2)
    v = qkv[..., 2 * GDN_KEY_DIM:].reshape(B, T, GDN_V_HEADS, GDN_HEAD_V).astype(jnp.float32)
    beta = jax.nn.sigmoid(b.astype(jnp.float32))
    g = -jnp.exp(a_log.astype(jnp.float32)) * jax.nn.softplus(a.astype(jnp.float32) + dt_bias.astype(jnp.float32))
    o, S = gated_delta_rule(q, k, v, g, beta, s0.astype(jnp.float32))
    o = rmsnorm(o, norm_w) * jax.nn.silu(z.reshape(B, T, GDN_V_HEADS, GDN_HEAD_V).astype(jnp.float32))
    y = o.astype(h.dtype).reshape(B, T, GDN_VAL_DIM) @ w_out
    return y, conv_state, S.astype(s0.dtype)


def rotary(x, pos):
    half = ROPE_DIM // 2
    inv = ROPE_THETA ** (-jnp.arange(half, dtype=jnp.float32) * 2.0 / ROPE_DIM)
    ang = pos.astype(jnp.float32)[:, None] * inv[None, :]
    cos = jnp.cos(ang)[None, :, None, None, :]
    sin = jnp.sin(ang)[None, :, None, None, :]
    xr = x[..., :ROPE_DIM].astype(jnp.float32)
    x1, x2 = xr[..., :half], xr[..., half:]
    rot = jnp.concatenate([x1 * cos - x2 * sin, x2 * cos + x1 * sin], axis=-1)
    return jnp.concatenate([rot.astype(x.dtype), x[..., ROPE_DIM:]], axis=-1)


def diff_project(h, w_in, q_norm, k_norm, pos):
    B, T, _ = h.shape
    qkv = h @ w_in
    q = qkv[..., :DIFF_QK_DIM].reshape(B, T, DIFF_HEADS, 2, DIFF_HEAD)
    k = qkv[..., DIFF_QK_DIM:2 * DIFF_QK_DIM].reshape(B, T, DIFF_HEADS, 2, DIFF_HEAD)
    v = qkv[..., 2 * DIFF_QK_DIM:].reshape(B, T, DIFF_HEADS, DIFF_V_HEAD)
    return rotary(rmsnorm(q, q_norm), pos), rotary(rmsnorm(k, k_norm), pos), v


def diff_lambda(lq1, lk1, lq2, lk2, lam_init):
    f32 = jnp.float32
    return (jnp.exp(jnp.sum(lq1.astype(f32) * lk1.astype(f32)))
            - jnp.exp(jnp.sum(lq2.astype(f32) * lk2.astype(f32))) + lam_init)


def diff_weights(s, lam):
    p = jax.nn.softmax(s, axis=-1)
    return p[:, :, 0] - lam * p[:, :, 1]


def diff_attn_prompt(q, k, v, lam):
    B, T = q.shape[:2]
    scale = DIFF_HEAD ** -0.5
    kpos = jnp.arange(T)

    def block(i):
        qb = lax.dynamic_slice_in_dim(q, i * Q_BLOCK, Q_BLOCK, axis=1)
        s = jnp.einsum("bqhcd,bkhcd->bhcqk", qb, k, preferred_element_type=jnp.float32) * scale
        qpos = i * Q_BLOCK + jnp.arange(Q_BLOCK)
        s = jnp.where(kpos[None, :] <= qpos[:, None], s, -jnp.inf)
        a = diff_weights(s, lam).astype(v.dtype)
        return jnp.einsum("bhqk,bkhe->bqhe", a, v)

    o = lax.map(block, jnp.arange(T // Q_BLOCK))
    return o.transpose(1, 0, 2, 3, 4).reshape(B, T, DIFF_HEADS, DIFF_V_HEAD)


def diff_attn_sample(q, k_new, v_new, k_past, v_past, lam):
    T = q.shape[1]
    P = k_past.shape[1]
    scale = DIFF_HEAD ** -0.5
    s_past = jnp.einsum("bqhcd,bkhcd->bhcqk", q, k_past, preferred_element_type=jnp.float32) * scale
    s_new = jnp.einsum("bqhcd,bkhcd->bhcqk", q, k_new, preferred_element_type=jnp.float32) * scale
    s_new = jnp.where(jnp.tril(jnp.ones((T, T), bool)), s_new, -jnp.inf)
    a = diff_weights(jnp.concatenate([s_past, s_new], axis=-1), lam).astype(v_new.dtype)
    return (jnp.einsum("bhqk,bkhe->bqhe", a[..., :P], v_past)
            + jnp.einsum("bhqk,bkhe->bqhe", a[..., P:], v_new))


def diff_out(o, sub_norm, lam_init, w_out):
    B, T = o.shape[:2]
    o = rmsnorm(o, sub_norm) * (1.0 - lam_init)
    return o.reshape(B, T, DIFF_V_DIM) @ w_out


def setup_inputs(seed: int = 0) -> dict:
    key = jax.random.key(seed)
    ks = iter(jax.random.split(key, 48))

    def nrm(shape, scale):
        return jax.random.normal(next(ks), shape, jnp.float32) * scale

    def gain(shape):
        return 1.0 + nrm(shape, 0.02)

    n_pages = PAST_LEN // PAGE_SIZE
    n_used = DEC_BATCH * n_pages
    n_pool = n_used + n_used // 4
    page_table = jax.random.permutation(next(ks), n_pool)[:n_used].reshape(DEC_BATCH, n_pages).astype(jnp.int32)
    return {
        "x_prompt": nrm((BATCH, SEQ, D_MODEL), 1.0),
        "x_sample": nrm((DEC_BATCH, DEC_SEQ, D_MODEL), 1.0),
        "state_delta": nrm((N_A_LAYERS, DEC_BATCH, GDN_V_HEADS, GDN_HEAD_K, GDN_HEAD_V), 0.1),
        "state_conv": nrm((N_A_LAYERS, DEC_BATCH, GDN_CONV - 1, GDN_CONV_DIM), 1.0),
        "cache_k": nrm((N_B_LAYERS, n_pool, PAGE_SIZE, DIFF_HEADS, 2, DIFF_HEAD), 1.0),
        "cache_v": nrm((N_B_LAYERS, n_pool, PAGE_SIZE, DIFF_HEADS, DIFF_V_HEAD), 1.0),
        "page_table": page_table,
        "norm_ffn1": gain((DEPTH, D_MODEL)),
        "ffn1_w_gu": nrm((DEPTH, D_MODEL, 2 * D_FF), D_MODEL ** -0.5),
        "ffn1_w_dn": nrm((DEPTH, D_FF, D_MODEL), D_FF ** -0.5),
        "norm_mix": gain((DEPTH, D_MODEL)),
        "norm_ffn2": gain((DEPTH, D_MODEL)),
        "ffn2_w_gu": nrm((DEPTH, D_MODEL, 2 * D_FF), D_MODEL ** -0.5),
        "ffn2_w_dn": nrm((DEPTH, D_FF, D_MODEL), D_FF ** -0.5),
        "gdn_w_in": nrm((N_A_LAYERS, D_MODEL, GDN_IN_DIM), D_MODEL ** -0.5),
        "gdn_conv_w": nrm((N_A_LAYERS, GDN_CONV, GDN_CONV_DIM), GDN_CONV ** -0.5),
        "gdn_a_log": jnp.log(jax.random.uniform(next(ks), (N_A_LAYERS, GDN_V_HEADS), jnp.float32, 0.5, 16.0)),
        "gdn_dt_bias": nrm((N_A_LAYERS, GDN_V_HEADS), 0.1),
        "gdn_norm_w": gain((N_A_LAYERS, GDN_HEAD_V)),
        "gdn_w_out": nrm((N_A_LAYERS, GDN_VAL_DIM, D_MODEL), GDN_VAL_DIM ** -0.5),
        "diff_w_in": nrm((N_B_LAYERS, D_MODEL, DIFF_IN_DIM), D_MODEL ** -0.5),
        "diff_q_norm": gain((N_B_LAYERS, DIFF_HEAD)),
        "diff_k_norm": gain((N_B_LAYERS, DIFF_HEAD)),
        "diff_lam_q1": nrm((N_B_LAYERS, DIFF_HEAD), 0.1),
        "diff_lam_k1": nrm((N_B_LAYERS, DIFF_HEAD), 0.1),
        "diff_lam_q2": nrm((N_B_LAYERS, DIFF_HEAD), 0.1),
        "diff_lam_k2": nrm((N_B_LAYERS, DIFF_HEAD), 0.1),
        "diff_sub_norm": gain((N_B_LAYERS, DIFF_V_HEAD)),
        "diff_w_out": nrm((N_B_LAYERS, DIFF_V_DIM, D_MODEL), DIFF_V_DIM ** -0.5),
    }


def reference(x_prompt, x_sample, state_delta, state_conv, cache_k, cache_v, page_table,
              norm_ffn1, ffn1_w_gu, ffn1_w_dn, norm_mix, norm_ffn2, ffn2_w_gu, ffn2_w_dn,
              gdn_w_in, gdn_conv_w, gdn_a_log, gdn_dt_bias, gdn_norm_w, gdn_w_out,
              diff_w_in, diff_q_norm, diff_k_norm, diff_lam_q1, diff_lam_k1, diff_lam_q2, diff_lam_k2,
              diff_sub_norm, diff_w_out):
    Bp, Tp, _ = x_prompt.shape
    Bs, Ts, _ = x_sample.shape
    n_pages = page_table.shape[1]
    past = n_pages * PAGE_SIZE
    pos_p = jnp.arange(Tp)
    pos_s = past + jnp.arange(Ts)
    conv0 = jnp.zeros((Bp, GDN_CONV - 1, GDN_CONV_DIM), x_prompt.dtype)
    s0 = jnp.zeros((Bp, GDN_V_HEADS, GDN_HEAD_K, GDN_HEAD_V), jnp.float32)
    p_delta, p_conv, p_k, p_v = [], [], [], []
    s_delta, s_conv, s_k, s_v = [], [], [], []
    xp, xs = x_prompt, x_sample
    for i in range(DEPTH):
        xp = half_ffn(xp, norm_ffn1[i], ffn1_w_gu[i], ffn1_w_dn[i])
        xs = half_ffn(xs, norm_ffn1[i], ffn1_w_gu[i], ffn1_w_dn[i])
        hp = rmsnorm(xp, norm_mix[i])
        hs = rmsnorm(xs, norm_mix[i])
        j = i // N_MIXERS
        if i % N_MIXERS == 0:
            wa = (gdn_w_in[j], gdn_conv_w[j], gdn_a_log[j], gdn_dt_bias[j], gdn_norm_w[j], gdn_w_out[j])
            yp, cp, sp = gdn_mixer(hp, conv0, s0, *wa)
            ys, cs, ss = gdn_mixer(hs, state_conv[j], state_delta[j], *wa)
            p_conv.append(cp)
            p_delta.append(sp)
            s_conv.append(cs)
            s_delta.append(ss)
        else:
            lam_init = 0.8 - 0.6 * math.exp(-0.3 * i)
            lam = diff_lambda(diff_lam_q1[j], diff_lam_k1[j], diff_lam_q2[j], diff_lam_k2[j], lam_init)
            qp, kp, vp = diff_project(hp, diff_w_in[j], diff_q_norm[j], diff_k_norm[j], pos_p)
            qs, kn, vn = diff_project(hs, diff_w_in[j], diff_q_norm[j], diff_k_norm[j], pos_s)
            k_past = cache_k[j][page_table].reshape(Bs, past, DIFF_HEADS, 2, DIFF_HEAD)
            v_past = cache_v[j][page_table].reshape(Bs, past, DIFF_HEADS, DIFF_V_HEAD)
            yp = diff_out(diff_attn_prompt(qp, kp, vp, lam), diff_sub_norm[j], lam_init, diff_w_out[j])
            ys = diff_out(diff_attn_sample(qs, kn, vn, k_past, v_past, lam), diff_sub_norm[j], lam_init, diff_w_out[j])
            p_k.append(kp)
            p_v.append(vp)
            s_k.append(kn)
            s_v.append(vn)
        xp = xp + yp
        xs = xs + ys
        xp = half_ffn(xp, norm_ffn2[i], ffn2_w_gu[i], ffn2_w_dn[i])
        xs = half_ffn(xs, norm_ffn2[i], ffn2_w_gu[i], ffn2_w_dn[i])
    return (xp, xs,
            jnp.stack(p_delta), jnp.stack(p_conv), jnp.stack(p_k), jnp.stack(p_v),
            jnp.stack(s_delta), jnp.stack(s_conv), jnp.stack(s_k), jnp.stack(s_v))
```

```python
import functools
import math

import jax
import jax.numpy as jnp
from jax import lax
from jax.experimental import pallas as pl
from jax.experimental.pallas import tpu as pltpu

F32 = jnp.float32
BF16 = jnp.bfloat16
EPS = 1e-6

D_MODEL = 2048
D_FF = 5632
GDN_QK_HEADS = 16
GDN_V_HEADS = 32
GDN_HEAD = 128
GDN_CONV = 4
GDN_CHUNK = 64
GDN_KEY_DIM = GDN_QK_HEADS * GDN_HEAD
GDN_VAL_DIM = GDN_V_HEADS * GDN_HEAD
GDN_CONV_DIM = 2 * GDN_KEY_DIM + GDN_VAL_DIM
GDN_IN_DIM = GDN_CONV_DIM + GDN_VAL_DIM + 2 * GDN_V_HEADS
DIFF_HEADS = 8
DIFF_HEAD = 128
DIFF_V_HEAD = 2 * DIFF_HEAD
DIFF_QK_DIM = DIFF_HEADS * 2 * DIFF_HEAD
DIFF_V_DIM = DIFF_HEADS * DIFF_V_HEAD
ROPE_DIM = DIFF_HEAD // 4
ROPE_THETA = 500000.0
PAGE_SIZE = 128

V7X_LANES = 128
V7X_VMEM_LIMIT_BYTES = 56 * 1024 * 1024


def _cparams(semantics, vmem_bytes):
    return pltpu.CompilerParams(dimension_semantics=semantics,
                                vmem_limit_bytes=int(min(max(vmem_bytes, 16 * 2**20), V7X_VMEM_LIMIT_BYTES)))


def _dot(a, b):
    return jnp.dot(a, b, preferred_element_type=F32)


def _dot_nt(a, b):
    return lax.dot_general(a, b, (((1,), (1,)), ((), ())), preferred_element_type=F32)


def _dot_tn(a, b):
    return lax.dot_general(a, b, (((0,), (0,)), ((), ())), preferred_element_type=F32)


def _rms_rows(x):
    return x * lax.rsqrt(jnp.mean(x * x, axis=-1, keepdims=True) + EPS)


def _silu(x):
    return x * jax.nn.sigmoid(x)


def _ffn_body(x_ref, g_ref, wg_ref, wu_ref, wd_ref, o_ref, xn_ref):
    j = pl.program_id(1)

    @pl.when(j == 0)
    def _():
        x = x_ref[...]
        xn_ref[...] = (_rms_rows(x) * g_ref[...]).astype(BF16)
        o_ref[...] = x

    xn = xn_ref[...]
    gate = _dot(xn, wg_ref[...])
    up = _dot(xn, wu_ref[...])
    act = (_silu(gate) * up).astype(BF16)
    o_ref[...] += 0.5 * _dot(act, wd_ref[...])


def _ffn(x, g_all, wgu_all, wdn_all, layer, tm, tf):
    m, d = x.shape
    f = wdn_all.shape[1]
    nf = f // tf
    vmem = 4 * tm * d * 4 + tm * d * 2 + 2 * (2 * d * tf + tf * d) * 2 + 4 * tm * tf * 4
    return pl.pallas_call(
        _ffn_body,
        grid=(m // tm, nf),
        in_specs=[
            pl.BlockSpec((tm, d), lambda i, j: (i, 0)),
            pl.BlockSpec((None, 1, d), lambda i, j: (layer, 0, 0)),
            pl.BlockSpec((None, d, tf), lambda i, j: (layer, 0, j)),
            pl.BlockSpec((None, d, tf), lambda i, j: (layer, 0, nf + j)),
            pl.BlockSpec((None, tf, d), lambda i, j: (layer, j, 0)),
        ],
        out_specs=pl.BlockSpec((tm, d), lambda i, j: (i, 0)),
        out_shape=jax.ShapeDtypeStruct((m, d), F32),
        scratch_shapes=[pltpu.VMEM((tm, d), BF16)],
        compiler_params=_cparams(("parallel", "arbitrary"), vmem),
        name="half_ffn",
    )(x, g_all, wgu_all, wgu_all, wdn_all)


def _proj_body(x_ref, g_ref, w_ref, o_ref, xn_ref):
    @pl.when(pl.program_id(1) == 0)
    def _():
        xn_ref[...] = (_rms_rows(x_ref[...]) * g_ref[...]).astype(BF16)

    o_ref[...] = _dot(xn_ref[...], w_ref[...])


def _proj(x, g_all, w_all, layer, w_layer, tm, tn):
    m, d = x.shape
    n = w_all.shape[2]
    vmem = 2 * tm * d * 4 + tm * d * 2 + 2 * d * tn * 2 + 3 * tm * tn * 4
    return pl.pallas_call(
        _proj_body,
        grid=(m // tm, n // tn),
        in_specs=[
            pl.BlockSpec((tm, d), lambda i, j: (i, 0)),
            pl.BlockSpec((None, 1, d), lambda i, j: (layer, 0, 0)),
            pl.BlockSpec((None, d, tn), lambda i, j: (w_layer, 0, j)),
        ],
        out_specs=pl.BlockSpec((tm, tn), lambda i, j: (i, j)),
        out_shape=jax.ShapeDtypeStruct((m, n), F32),
        scratch_shapes=[pltpu.VMEM((tm, d), BF16)],
        compiler_params=_cparams(("parallel", "arbitrary"), vmem),
        name="norm_proj",
    )(x, g_all, w_all)


def _diffproj_body(x_ref, g_ref, w_ref, qn_ref, kn_ref, cos_ref, sa_ref, sb_ref,
                   q_ref, k_ref, v_ref, xn_ref, *, nq):
    j = pl.program_id(1)

    @pl.when(j == 0)
    def _():
        xn_ref[...] = (_rms_rows(x_ref[...]) * g_ref[...]).astype(BF16)

    acc = _dot(xn_ref[...], w_ref[...])
    heads = acc.shape[1] // DIFF_HEAD

    def norm_rope(dst_ref, wn):
        cos = cos_ref[...]
        sa = sa_ref[...]
        sb = sb_ref[...]
        for h in range(heads):
            y = _rms_rows(acc[:, h * DIFF_HEAD:(h + 1) * DIFF_HEAD]) * wn
            y = (y * cos + pltpu.roll(y, DIFF_HEAD - ROPE_DIM // 2, 1) * sa
                 + pltpu.roll(y, ROPE_DIM // 2, 1) * sb)
            dst_ref[:, h * DIFF_HEAD:(h + 1) * DIFF_HEAD] = y

    @pl.when(j < nq)
    def _():
        norm_rope(q_ref, qn_ref[...])

    @pl.when(jnp.logical_and(j >= nq, j < 2 * nq))
    def _():
        norm_rope(k_ref, kn_ref[...])

    @pl.when(j >= 2 * nq)
    def _():
        v_ref[...] = acc


def _diffproj(x, g_all, w_all, qn_all, kn_all, cos, sa, sb, layer, w_layer, tm, tn):
    m, d = x.shape
    nq = DIFF_QK_DIM // tn
    nv = DIFF_V_DIM // tn
    vmem = 2 * tm * d * 4 + tm * d * 2 + 2 * d * tn * 2 + 8 * tm * tn * 4 + 6 * tm * 128 * 4
    out = jax.ShapeDtypeStruct((m, DIFF_QK_DIM), F32)
    return pl.pallas_call(
        functools.partial(_diffproj_body, nq=nq),
        grid=(m // tm, 2 * nq + nv),
        in_specs=[
            pl.BlockSpec((tm, d), lambda i, j: (i, 0)),
            pl.BlockSpec((None, 1, d), lambda i, j: (layer, 0, 0)),
            pl.BlockSpec((None, d, tn), lambda i, j: (w_layer, 0, j)),
            pl.BlockSpec((None, 1, DIFF_HEAD), lambda i, j: (w_layer, 0, 0)),
            pl.BlockSpec((None, 1, DIFF_HEAD), lambda i, j: (w_layer, 0, 0)),
            pl.BlockSpec((tm, DIFF_HEAD), lambda i, j: (i, 0)),
            pl.BlockSpec((tm, DIFF_HEAD), lambda i, j: (i, 0)),
            pl.BlockSpec((tm, DIFF_HEAD), lambda i, j: (i, 0)),
        ],
        out_specs=[
            pl.BlockSpec((tm, tn), lambda i, j: (i, jnp.minimum(j, nq - 1))),
            pl.BlockSpec((tm, tn), lambda i, j: (i, jnp.clip(j - nq, 0, nq - 1))),
            pl.BlockSpec((tm, tn), lambda i, j: (i, jnp.clip(j - 2 * nq, 0, nv - 1))),
        ],
        out_shape=[out, out, jax.ShapeDtypeStruct((m, DIFF_V_DIM), F32)],
        scratch_shapes=[pltpu.VMEM((tm, d), BF16)],
        compiler_params=_cparams(("parallel", "arbitrary"), vmem),
        name="diff_proj",
    )(x, g_all, w_all, qn_all, kn_all, cos, sa, sb)


def _outproj_body(y_ref, w_ref, x_ref, o_ref):
    o_ref[...] = x_ref[...] + _dot(y_ref[...], w_ref[...])


def _outproj(y, w_all, x, w_layer, tm, tn):
    m, k = y.shape
    d = x.shape[1]
    vmem = 2 * tm * k * 2 + 2 * k * tn * 2 + 5 * tm * tn * 4
    return pl.pallas_call(
        _outproj_body,
        grid=(m // tm, d // tn),
        in_specs=[
            pl.BlockSpec((tm, k), lambda i, j: (i, 0)),
            pl.BlockSpec((None, k, tn), lambda i, j: (w_layer, 0, j)),
            pl.BlockSpec((tm, tn), lambda i, j: (i, j)),
        ],
        out_specs=pl.BlockSpec((tm, tn), lambda i, j: (i, j)),
        out_shape=jax.ShapeDtypeStruct((m, d), F32),
        compiler_params=_cparams(("parallel", "arbitrary"), vmem),
        name="out_proj",
    )(y, w_all, x)


def _unit_lower_inverse_minus_identity(mm, level_masks, base_mask):
    y = -(mm * base_mask)
    for mask in level_masks:
        o = mm * mask
        y_bf = y.astype(BF16)
        z = o + _dot(y_bf, o.astype(BF16))
        y = y - z - _dot(z.astype(BF16), y_bf)
    return y


def _gdn_body(qp_ref, kp_ref, vp_ref, z_ref, q8_ref, k8_ref, v8_ref, cwq_ref, cwk_ref, cwv_ref,
              b_ref, a_ref, alog_ref, dtb_ref, nw_ref, s0_ref,
              o_ref, sout_ref,
              q_s, k_s, v_s, u_s, w_s, at_s, grow_s, gcc_s, bc_s, glc_s,
              *, t_len, t_valid, group):
    C = GDN_CHUNK
    HD = GDN_HEAD
    nc = t_len // C
    rt = min(256, t_len)
    n_rt = t_len // rt

    def conv_tile(x_ref, p8_ref, cw_ref, r):
        head = p8_ref[...] if r == 0 else x_ref[r * rt - 8:r * rt, :]
        cur = x_ref[r * rt:(r + 1) * rt, :]
        ext = jnp.concatenate([head, cur], axis=0)
        w = cw_ref[...]
        y = ext[5:5 + rt, :] * w[0:1, :]
        y = y + ext[6:6 + rt, :] * w[1:2, :]
        y = y + ext[7:7 + rt, :] * w[2:3, :]
        y = y + cur * w[3:4, :]
        y = _silu(y)
        if t_valid < t_len:
            rows = r * rt + lax.broadcasted_iota(jnp.int32, y.shape, 0)
            y = jnp.where(rows < t_valid, y, 0.0)
        return y

    def l2n(x):
        return x * lax.rsqrt(jnp.sum(x * x, axis=-1, keepdims=True) + EPS)

    for r in range(n_rt):
        rows = slice(r * rt, (r + 1) * rt)
        q_s[rows, :] = l2n(conv_tile(qp_ref, q8_ref, cwq_ref, r)) * (HD ** -0.5)
        k_s[rows, :] = l2n(conv_tile(kp_ref, k8_ref, cwk_ref, r))
        v_s[rows, :] = conv_tile(vp_ref, v8_ref, cwv_ref, r)

    ri = lax.broadcasted_iota(jnp.int32, (C, C), 0)
    ci = lax.broadcasted_iota(jnp.int32, (C, C), 1)
    lower = ci <= ri
    strict = ci < ri
    upper_ones = (ri <= ci).astype(F32)
    eye = (ri == ci).astype(F32)
    hi = lax.Precision.HIGHEST
    for h in range(2):
        beta = jax.nn.sigmoid(b_ref[h])
        g = -jnp.exp(alog_ref[h]) * jax.nn.softplus(a_ref[h] + dtb_ref[h])
        if t_valid < t_len:
            pos = (lax.broadcasted_iota(jnp.int32, g.shape, 0) * C
                   + lax.broadcasted_iota(jnp.int32, g.shape, 1))
            g = jnp.where(pos < t_valid, g, 0.0)
            beta = jnp.where(pos < t_valid, beta, 0.0)
        gc = jnp.dot(g, upper_ones, precision=hi, preferred_element_type=F32)
        grow_s[h] = gc
        gc_t = lax.dot_general(eye, gc, (((1,), (1,)), ((), ())), precision=hi,
                               preferred_element_type=F32)
        beta_t = lax.dot_general(eye, beta, (((1,), (1,)), ((), ())), precision=hi,
                                 preferred_element_type=F32)
        for c in range(nc):
            gcol = gc_t[:, c:c + 1]
            gcc_s[h, c] = jnp.broadcast_to(gcol, (C, HD))
            bc_s[h, c] = jnp.broadcast_to(beta_t[:, c:c + 1], (C, HD))
            glc_s[h, c] = jnp.broadcast_to(gc_t[C - 1:C, c:c + 1] - gcol, (C, HD))

    base_mask = ((ri >> 1) == (ci >> 1)).astype(F32)
    level_masks = []
    for s in range(1, 6):
        same_big = (ri >> (s + 1)) == (ci >> (s + 1))
        same_small = (ri >> s) == (ci >> s)
        level_masks.append(jnp.where(same_big, 1.0, 0.0) - jnp.where(same_small, 1.0, 0.0))

    def solve_group(gi, carry):
        for gg in range(group):
            c = gi * group + gg
            r0 = pl.multiple_of(c * C, C)
            kc = k_s[pl.ds(r0, C), :]
            k_bf = kc.astype(BF16)
            qk = _dot_nt(q_s[pl.ds(r0, C), :].astype(BF16), k_bf)
            for h in range(2):
                gcc = gcc_s[h, c]
                gd = gcc[:, :C] - grow_s[h, pl.ds(c, 1), :]
                dec = jnp.where(lower, jnp.exp(jnp.where(lower, gd, 0.0)), 0.0)
                beta_c = bc_s[h, c]
                kb = kc * beta_c
                mm = jnp.where(strict, _dot_nt(kb.astype(BF16), k_bf) * dec, 0.0)
                y = _unit_lower_inverse_minus_identity(mm, level_masks, base_mask)
                vc = v_s[pl.ds(r0, C), h * HD:(h + 1) * HD]
                rhs = jnp.concatenate([vc * beta_c, kb * jnp.exp(gcc)], axis=1)
                uw = rhs + _dot(y.astype(BF16), rhs.astype(BF16))
                u_s[h, pl.ds(r0, C), :] = uw[:, :HD]
                w_s[h, pl.ds(r0, C), :] = uw[:, HD:].astype(BF16)
                at_s[h, pl.ds(r0, C), :] = (qk * dec).astype(BF16)
        return carry

    lax.fori_loop(0, nc // group, solve_group, 0)

    nw = nw_ref[...]

    def step(c, states):
        r0 = pl.multiple_of(c * C, C)
        kc = k_s[pl.ds(r0, C), :]
        qc = q_s[pl.ds(r0, C), :]
        new_states = []
        for h in range(2):
            s_prev = states[h]
            gcc = gcc_s[h, c]
            qe = (qc * jnp.exp(gcc)).astype(BF16)
            lhs = jnp.concatenate([w_s[h, pl.ds(r0, C), :], qe], axis=0)
            ws = _dot(lhs, s_prev.astype(BF16))
            v_new = u_s[h, pl.ds(r0, C), :] - ws[:C]
            vn_bf = v_new.astype(BF16)
            o = ws[C:] + _dot(at_s[h, pl.ds(r0, C), :], vn_bf)
            ke = (kc * jnp.exp(glc_s[h, c])).astype(BF16)
            s_new = s_prev * jnp.exp(gcc[C - 1:C, :]) + _dot_tn(ke, vn_bf)
            new_states.append(s_new)
            zc = z_ref[pl.ds(r0, C), h * HD:(h + 1) * HD]
            o_ref[pl.ds(r0, C), h * HD:(h + 1) * HD] = (_rms_rows(o) * nw * _silu(zc)).astype(BF16)
        return tuple(new_states)

    s_fin = lax.fori_loop(0, nc, step, (s0_ref[0], s0_ref[1]))
    sout_ref[0] = s_fin[0]
    sout_ref[1] = s_fin[1]


def _gdn_core(p, prev8, conv_w_all, b_rows, a_rows, alog_b, dtb_b, norm_w_all, s0, layer,
              batch, t_len, t_valid):
    C = GDN_CHUNK
    HD = GDN_HEAD
    nc = t_len // C
    nc8 = b_rows.shape[2]
    group = 4 if nc % 4 == 0 else 1
    kq = GDN_KEY_DIM // HD
    kv = GDN_KEY_DIM // (2 * HD)
    vmem = (2 * 6 * t_len * HD * 4 + 2 * t_len * 2 * HD * 2 + (4 + 2) * t_len * HD * 4
            + 2 * t_len * HD * 2 * 2 + 3 * 2 * nc * C * HD * 4 + 8 * 2**20)
    grid = (batch, GDN_QK_HEADS)
    return pl.pallas_call(
        functools.partial(_gdn_body, t_len=t_len, t_valid=t_valid, group=group),
        grid=grid,
        in_specs=[
            pl.BlockSpec((t_len, HD), lambda b, h: (b, h)),
            pl.BlockSpec((t_len, HD), lambda b, h: (b, kq + h)),
            pl.BlockSpec((t_len, 2 * HD), lambda b, h: (b, kq + h)),
            pl.BlockSpec((t_len, 2 * HD), lambda b, h: (b, kq + GDN_V_HEADS // 2 + h)),
            pl.BlockSpec((None, 8, HD), lambda b, h: (b, 0, h)),
            pl.BlockSpec((None, 8, HD), lambda b, h: (b, 0, kq + h)),
            pl.BlockSpec((None, 8, 2 * HD), lambda b, h: (b, 0, kq + h)),
            pl.BlockSpec((None, GDN_CONV, HD), lambda b, h: (layer, 0, h)),
            pl.BlockSpec((None, GDN_CONV, HD), lambda b, h: (layer, 0, kq + h)),
            pl.BlockSpec((None, GDN_CONV, 2 * HD), lambda b, h: (layer, 0, kq + h)),
            pl.BlockSpec((None, 2, nc8, C), lambda b, h: (b, h, 0, 0)),
            pl.BlockSpec((None, 2, nc8, C), lambda b, h: (b, h, 0, 0)),
            pl.BlockSpec((None, 2, 1, C), lambda b, h: (layer, h, 0, 0)),
            pl.BlockSpec((None, 2, 1, C), lambda b, h: (layer, h, 0, 0)),
            pl.BlockSpec((None, 1, HD), lambda b, h: (layer, 0, 0)),
            pl.BlockSpec((None, 2, HD, HD), lambda b, h: (b, h, 0, 0)),
        ],
        out_specs=[
            pl.BlockSpec((t_len, 2 * HD), lambda b, h: (b, h)),
            pl.BlockSpec((None, 2, HD, HD), lambda b, h: (b, h, 0, 0)),
        ],
        out_shape=[
            jax.ShapeDtypeStruct((batch * t_len, GDN_VAL_DIM), BF16),
            jax.ShapeDtypeStruct((batch, GDN_V_HEADS, HD, HD), F32),
        ],
        scratch_shapes=[
            pltpu.VMEM((t_len, HD), F32),
            pltpu.VMEM((t_len, HD), F32),
            pltpu.VMEM((t_len, 2 * HD), F32),
            pltpu.VMEM((2, t_len, HD), F32),
            pltpu.VMEM((2, t_len, HD), BF16),
            pltpu.VMEM((2, t_len, C), BF16),
            pltpu.VMEM((2, nc8, C), F32),
            pltpu.VMEM((2, nc, C, HD), F32),
            pltpu.VMEM((2, nc, C, HD), F32),
            pltpu.VMEM((2, nc, C, HD), F32),
        ],
        compiler_params=_cparams(("parallel", "arbitrary"), vmem),
        name="gdn_core",
    )(p, p, p, p, prev8, prev8, prev8, conv_w_all, conv_w_all, conv_w_all,
      b_rows, a_rows, alog_b, dtb_b, norm_w_all, s0)


def _diff_lambda(lq1_ref, lk1_ref, lq2_ref, lk2_ref, lam_init):
    s1 = jnp.sum(lq1_ref[...] * lk1_ref[...], axis=-1, keepdims=True)
    s2 = jnp.sum(lq2_ref[...] * lk2_ref[...], axis=-1, keepdims=True)
    return jnp.exp(s1) - jnp.exp(s2) + lam_init


def _flash_body(q_ref, k_ref, v_ref, lq1_ref, lk1_ref, lq2_ref, lk2_ref, sn_ref, o_ref,
                m_s, l_s, acc_s, *, tq, lam_init):
    qi = pl.program_id(2)
    scale = DIFF_HEAD ** -0.5
    q = q_ref[...]
    q_maps = (q[:, :DIFF_HEAD].astype(BF16), q[:, DIFF_HEAD:].astype(BF16))
    m_s[...] = jnp.full(m_s.shape, -jnp.inf, F32)
    l_s[...] = jnp.zeros(l_s.shape, F32)
    acc_s[...] = jnp.zeros(acc_s.shape, F32)
    rows = qi * tq + lax.broadcasted_iota(jnp.int32, (tq, tq), 0)
    cols0 = lax.broadcasted_iota(jnp.int32, (tq, tq), 1)

    def body(kb, carry):
        r0 = pl.multiple_of(kb * tq, tq)
        k = k_ref[pl.ds(r0, tq), :]
        v = v_ref[pl.ds(r0, tq), :].astype(BF16)
        mask = (cols0 + r0) <= rows
        for c in range(2):
            kc = k[:, c * DIFF_HEAD:(c + 1) * DIFF_HEAD].astype(BF16)
            s = _dot_nt(q_maps[c], kc) * scale
            s = jnp.where(mask, s, -jnp.inf)
            m_prev = m_s[c]
            m_new = jnp.maximum(m_prev, jnp.max(s, axis=-1, keepdims=True))
            alpha = jnp.exp(m_prev - m_new)
            p = jnp.exp(s - m_new)
            l_s[c] = alpha * l_s[c] + jnp.sum(p, axis=-1, keepdims=True)
            acc_s[c] = alpha * acc_s[c] + _dot(p.astype(BF16), v)
            m_s[c] = m_new
        return carry

    lax.fori_loop(0, qi + 1, body, 0)
    lam = _diff_lambda(lq1_ref, lk1_ref, lq2_ref, lk2_ref, lam_init)
    o = acc_s[0] / l_s[0] - lam * (acc_s[1] / l_s[1])
    o_ref[...] = (_rms_rows(o) * sn_ref[...] * (1.0 - lam_init)).astype(BF16)


def _flash(q, k, v, lq1, lk1, lq2, lk2, sn_all, w_layer, lam_init, batch, t_len, tq):
    nq = t_len // tq
    dv = DIFF_V_HEAD
    vmem = 2 * tq * dv * 4 + 4 * t_len * dv * 4 + 2 * tq * dv * 2 + 2 * tq * dv * 4 + 8 * tq * tq * 4 + 4 * 2**20
    lam_spec = pl.BlockSpec((None, 1, DIFF_HEAD), lambda b, h, i: (w_layer, 0, 0))
    return pl.pallas_call(
        functools.partial(_flash_body, tq=tq, lam_init=lam_init),
        grid=(batch, DIFF_HEADS, nq),
        in_specs=[
            pl.BlockSpec((tq, dv), lambda b, h, i: (b * nq + i, h)),
            pl.BlockSpec((t_len, dv), lambda b, h, i: (b, h)),
            pl.BlockSpec((t_len, dv), lambda b, h, i: (b, h)),
            lam_spec, lam_spec, lam_spec, lam_spec,
            pl.BlockSpec((None, 1, dv), lambda b, h, i: (w_layer, 0, 0)),
        ],
        out_specs=pl.BlockSpec((tq, dv), lambda b, h, i: (b * nq + i, h)),
        out_shape=jax.ShapeDtypeStruct((batch * t_len, DIFF_V_DIM), BF16),
        scratch_shapes=[
            pltpu.VMEM((2, tq, 1), F32),
            pltpu.VMEM((2, tq, 1), F32),
            pltpu.VMEM((2, tq, dv), F32),
        ],
        compiler_params=_cparams(("parallel", "parallel", "arbitrary"), vmem),
        name="diff_flash",
    )(q, k, v, lq1, lk1, lq2, lk2, sn_all)


def _decode_body(pt_ref, q_ref, kn_ref, vn_ref, kc_ref, vc_ref, lq1_ref, lk1_ref, lq2_ref, lk2_ref,
                 sn_ref, o_ref, qrow_s, m_s, l_s, acc_s, *, n_pages, n_new, lam_init):
    del pt_ref
    p = pl.program_id(1)
    scale = DIFF_HEAD ** -0.5
    n_rows = qrow_s.shape[0]
    rows_per_head = n_rows // DIFF_HEADS

    @pl.when(p == 0)
    def _():
        rown = lax.broadcasted_iota(jnp.int32, qrow_s.shape, 0) // n_new
        colm = lax.broadcasted_iota(jnp.int32, qrow_s.shape, 1) // DIFF_HEAD
        qrow_s[...] = jnp.where(rown == colm, q_ref[...], 0.0).astype(BF16)
        m_s[...] = jnp.full(m_s.shape, -jnp.inf, F32)
        l_s[...] = jnp.zeros(l_s.shape, F32)
        acc_s[...] = jnp.zeros(acc_s.shape, F32)

    def update(k_page, v_page, mask):
        s = _dot_nt(qrow_s[...], k_page.astype(BF16)) * scale
        if mask is not None:
            s = jnp.where(mask, s, -jnp.inf)
        m_prev = m_s[...]
        m_new = jnp.maximum(m_prev, jnp.max(s, axis=-1, keepdims=True))
        alpha = jnp.exp(m_prev - m_new)
        pr = jnp.exp(s - m_new)
        l_s[...] = alpha * l_s[...] + jnp.sum(pr, axis=-1, keepdims=True)
        m_s[...] = m_new
        pr_bf = pr.astype(BF16)
        v_bf = v_page.astype(BF16)
        for h in range(DIFF_HEADS):
            rs = slice(h * rows_per_head, (h + 1) * rows_per_head)
            pv = _dot(pr_bf[rs, :], v_bf[:, h * DIFF_V_HEAD:(h + 1) * DIFF_V_HEAD])
            acc_s[rs, :] = alpha[rs, :] * acc_s[rs, :] + pv

    update(kc_ref[...], vc_ref[...], None)

    @pl.when(p == n_pages - 1)
    def _():
        key = lax.broadcasted_iota(jnp.int32, (n_rows, PAGE_SIZE), 1)
        qidx = lax.broadcasted_iota(jnp.int32, (n_rows, PAGE_SIZE), 0) % n_new
        update(kn_ref[...], vn_ref[...], key <= qidx)
        lam = _diff_lambda(lq1_ref, lk1_ref, lq2_ref, lk2_ref, lam_init)
        o = acc_s[...] / l_s[...]
        n_out = n_rows // 2
        ro = lax.broadcasted_iota(jnp.int32, (n_out, n_rows), 0)
        co = lax.broadcasted_iota(jnp.int32, (n_out, n_rows), 1)
        base = (ro // n_new) * (2 * n_new) + ro % n_new
        sel = jnp.where(co == base, 1.0, 0.0) - lam * jnp.where(co == base + n_new, 1.0, 0.0)
        od = jnp.dot(sel, o, precision=lax.Precision.HIGHEST, preferred_element_type=F32)
        o_ref[...] = _rms_rows(od) * sn_ref[...] * (1.0 - lam_init)


def _decode(page_table, q_tiled, kn_pad, vn_pad, cache_k, cache_v, lq1, lk1, lq2, lk2, sn_all,
            w_layer, lam_init, n_new):
    bs, n_pages = page_table.shape
    n_rows = q_tiled.shape[1]
    feat = DIFF_QK_DIM
    lam_spec = pl.BlockSpec((None, 1, DIFF_HEAD), lambda b, p, pt: (w_layer, 0, 0))
    vmem = 4 * PAGE_SIZE * feat * 4 + 4 * PAGE_SIZE * feat * 4 + 2 * n_rows * feat * 4 + 16 * 2**20
    grid_spec = pltpu.PrefetchScalarGridSpec(
        num_scalar_prefetch=1,
        grid=(bs, n_pages),
        in_specs=[
            pl.BlockSpec((None, n_rows, feat), lambda b, p, pt: (b, 0, 0)),
            pl.BlockSpec((None, PAGE_SIZE, feat), lambda b, p, pt: (b, 0, 0)),
            pl.BlockSpec((None, PAGE_SIZE, DIFF_V_DIM), lambda b, p, pt: (b, 0, 0)),
            pl.BlockSpec((None, None, PAGE_SIZE, feat), lambda b, p, pt: (w_layer, pt[b, p], 0, 0)),
            pl.BlockSpec((None, None, PAGE_SIZE, DIFF_V_DIM), lambda b, p, pt: (w_layer, pt[b, p], 0, 0)),
            lam_spec, lam_spec, lam_spec, lam_spec,
            pl.BlockSpec((None, 1, DIFF_V_HEAD), lambda b, p, pt: (w_layer, 0, 0)),
        ],
        out_specs=pl.BlockSpec((None, n_rows // 2, DIFF_V_HEAD), lambda b, p, pt: (b, 0, 0)),
        scratch_shapes=[
            pltpu.VMEM((n_rows, feat), BF16),
            pltpu.VMEM((n_rows, 1), F32),
            pltpu.VMEM((n_rows, 1), F32),
            pltpu.VMEM((n_rows, DIFF_V_HEAD), F32),
        ],
    )
    return pl.pallas_call(
        functools.partial(_decode_body, n_pages=n_pages, n_new=n_new, lam_init=lam_init),
        grid_spec=grid_spec,
        out_shape=jax.ShapeDtypeStruct((bs, n_rows // 2, DIFF_V_HEAD), F32),
        compiler_params=_cparams(("parallel", "arbitrary"), vmem),
        name="diff_decode",
    )(page_table, q_tiled, kn_pad, vn_pad, cache_k, cache_v, lq1, lk1, lq2, lk2, sn_all)


def _rope_tables(pos):
    half = ROPE_DIM // 2
    inv = ROPE_THETA ** (-jnp.arange(half, dtype=F32) * 2.0 / ROPE_DIM)
    ang = pos.astype(F32)[:, None] * inv[None, :]
    cos, sin = jnp.cos(ang), jnp.sin(ang)
    n = pos.shape[0]
    tail = DIFF_HEAD - ROPE_DIM
    c = jnp.concatenate([cos, cos, jnp.ones((n, tail), F32)], axis=1)
    sa = jnp.concatenate([-sin, jnp.zeros((n, half + tail), F32)], axis=1)
    sb = jnp.concatenate([jnp.zeros((n, half), F32), sin, jnp.zeros((n, tail), F32)], axis=1)
    return c, sa, sb


def _gate_rows(p, batch, t_len):
    C = GDN_CHUNK
    nc = t_len // C
    ba = p[:, GDN_CONV_DIM + GDN_VAL_DIM:GDN_CONV_DIM + GDN_VAL_DIM + 2 * GDN_V_HEADS]
    ba = ba.reshape(batch, nc, C, 2, GDN_V_HEADS).transpose(3, 0, 4, 1, 2)
    nc8 = -(-nc // 8) * 8
    ba = jnp.pad(ba, ((0, 0), (0, 0), (0, 0), (0, nc8 - nc), (0, 0)))
    return ba[0], ba[1]


def kernel(x_prompt, x_sample, state_delta, state_conv, cache_k, cache_v, page_table, norm_ffn1, ffn1_w_gu, ffn1_w_dn, norm_mix, norm_ffn2, ffn2_w_gu, ffn2_w_dn, gdn_w_in, gdn_conv_w, gdn_a_log, gdn_dt_bias, gdn_norm_w, gdn_w_out, diff_w_in, diff_q_norm, diff_k_norm, diff_lam_q1, diff_lam_k1, diff_lam_q2, diff_lam_k2, diff_sub_norm, diff_w_out):
    bp, tp, d = x_prompt.shape
    bs, ts, _ = x_sample.shape
    depth = norm_ffn1.shape[0]
    n_pages = page_table.shape[1]
    past = n_pages * PAGE_SIZE
    C = GDN_CHUNK

    bf = lambda w: w.astype(BF16)
    row = lambda w: w.reshape(w.shape[0], 1, w.shape[1])
    w_gu1, w_dn1, w_gu2, w_dn2 = bf(ffn1_w_gu), bf(ffn1_w_dn), bf(ffn2_w_gu), bf(ffn2_w_dn)
    tn = 512
    gdn_n = -(-GDN_IN_DIM // tn) * tn
    w_gin = jnp.pad(bf(gdn_w_in), ((0, 0), (0, 0), (0, gdn_n - GDN_IN_DIM)))
    w_gout, w_din, w_dout = bf(gdn_w_out), bf(diff_w_in), bf(diff_w_out)
    g_ffn1, g_mix, g_ffn2 = row(norm_ffn1), row(norm_mix), row(norm_ffn2)
    alog_b = jnp.broadcast_to(gdn_a_log[:, :, None, None], gdn_a_log.shape + (1, C))
    dtb_b = jnp.broadcast_to(gdn_dt_bias[:, :, None, None], gdn_dt_bias.shape + (1, C))
    gdn_nw = row(gdn_norm_w)
    qn, kn_w = row(diff_q_norm), row(diff_k_norm)
    lq1, lk1, lq2, lk2 = row(diff_lam_q1), row(diff_lam_k1), row(diff_lam_q2), row(diff_lam_k2)
    sub_n = row(diff_sub_norm)
    ck = cache_k.reshape(cache_k.shape[0], cache_k.shape[1], PAGE_SIZE, DIFF_QK_DIM)
    cv = cache_v.reshape(cache_v.shape[0], cache_v.shape[1], PAGE_SIZE, DIFF_V_DIM)

    rope_p = _rope_tables(jnp.arange(tp))
    rope_p = tuple(jnp.tile(t, (bp, 1)) for t in rope_p)
    rope_s = _rope_tables(past + jnp.arange(ts))
    rope_s = tuple(jnp.tile(t, (bs, 1)) for t in rope_s)

    xp = x_prompt.reshape(bp * tp, d)
    xs = x_sample.reshape(bs * ts, d)
    mp, ms = bp * tp, bs * ts
    tm_p, tm_s = 512, ms
    tf = 512
    ts_pad = -(-ts // C) * C

    prev8_p = jnp.zeros((bp, 8, GDN_CONV_DIM), F32)
    s0_p = jnp.zeros((bp, GDN_V_HEADS, GDN_HEAD, GDN_HEAD), F32)

    p_delta, p_conv, p_k, p_v = [], [], [], []
    s_delta, s_conv, s_k, s_v = [], [], [], []
    for i in range(depth):
        xp = _ffn(xp, g_ffn1, w_gu1, w_dn1, i, tm_p, tf)
        xs = _ffn(xs, g_ffn1, w_gu1, w_dn1, i, tm_s, tf)
        j = i // 2
        if i % 2 == 0:
            pp = _proj(xp, g_mix, w_gin, i, j, tm_p, tn)
            ps = _proj(xs, g_mix, w_gin, i, j, tm_s, tn)
            b_rows, a_rows = _gate_rows(pp, bp, tp)
            op, sp = _gdn_core(pp, prev8_p, gdn_conv_w, b_rows, a_rows, alog_b, dtb_b, gdn_nw, s0_p, j,
                               bp, tp, tp)
            p_conv.append(pp.reshape(bp, tp, gdn_n)[:, tp - (GDN_CONV - 1):, :GDN_CONV_DIM])
            p_delta.append(sp)
            xp = _outproj(op, w_gout, xp, j, tm_p, tn)

            ps3 = ps.reshape(bs, ts, gdn_n)
            ps_pad = jnp.pad(ps3, ((0, 0), (0, ts_pad - ts), (0, 0))).reshape(bs * ts_pad, gdn_n)
            prev8_s = jnp.pad(state_conv[j], ((0, 0), (8 - (GDN_CONV - 1), 0), (0, 0)))
            b_rows, a_rows = _gate_rows(ps_pad, bs, ts_pad)
            os_, ss = _gdn_core(ps_pad, prev8_s, gdn_conv_w, b_rows, a_rows, alog_b, dtb_b, gdn_nw,
                                state_delta[j], j, bs, ts_pad, ts)
            xp_conv = jnp.concatenate([state_conv[j], ps3[:, :, :GDN_CONV_DIM]], axis=1)
            s_conv.append(xp_conv[:, ts:])
            s_delta.append(ss)
            os_ = os_.reshape(bs, ts_pad, GDN_VAL_DIM)[:, :ts].reshape(ms, GDN_VAL_DIM)
            xs = _outproj(os_, w_gout, xs, j, tm_s, tn)
        else:
            lam_init = 0.8 - 0.6 * math.exp(-0.3 * i)
            qp, kp, vp = _diffproj(xp, g_mix, w_din, qn, kn_w, *rope_p, i, j, tm_p, tn)
            qs, ks, vs = _diffproj(xs, g_mix, w_din, qn, kn_w, *rope_s, i, j, tm_s, tn)
            op = _flash(qp, kp, vp, lq1, lk1, lq2, lk2, sub_n, j, lam_init, bp, tp, 512)
            xp = _outproj(op, w_dout, xp, j, tm_p, tn)
            p_k.append(kp.reshape(bp, tp, DIFF_HEADS, 2, DIFF_HEAD))
            p_v.append(vp.reshape(bp, tp, DIFF_HEADS, DIFF_V_HEAD))

            q_tiled = jnp.tile(qs.reshape(bs, ts, DIFF_QK_DIM), (1, 2 * DIFF_HEADS, 1))
            q_tiled = q_tiled.reshape(bs, 2 * DIFF_HEADS, ts, DIFF_QK_DIM).reshape(bs, 2 * DIFF_HEADS * ts, DIFF_QK_DIM)
            kn_pad = jnp.pad(ks.reshape(bs, ts, DIFF_QK_DIM), ((0, 0), (0, PAGE_SIZE - ts), (0, 0)))
            vn_pad = jnp.pad(vs.reshape(bs, ts, DIFF_V_DIM), ((0, 0), (0, PAGE_SIZE - ts), (0, 0)))
            od = _decode(page_table, q_tiled, kn_pad, vn_pad, ck, cv, lq1, lk1, lq2, lk2, sub_n,
                         j, lam_init, ts)
            od = od.reshape(bs, DIFF_HEADS, ts, DIFF_V_HEAD).transpose(0, 2, 1, 3).reshape(ms, DIFF_V_DIM)
            xs = _outproj(od.astype(BF16), w_dout, xs, j, tm_s, tn)
            s_k.append(ks.reshape(bs, ts, DIFF_HEADS, 2, DIFF_HEAD))
            s_v.append(vs.reshape(bs, ts, DIFF_HEADS, DIFF_V_HEAD))
        xp = _ffn(xp, g_ffn2, w_gu2, w_dn2, i, tm_p, tf)
        xs = _ffn(xs, g_ffn2, w_gu2, w_dn2, i, tm_s, tf)

    return (xp.reshape(bp, tp, d), xs.reshape(bs, ts, d),
            jnp.stack(p_delta), jnp.stack(p_conv), jnp.stack(p_k), jnp.stack(p_v),
            jnp.stack(s_delta), jnp.stack(s_conv), jnp.stack(s_k), jnp.stack(s_v))
```

```python
import functools
import math

import jax
import jax.numpy as jnp
from jax import lax
from jax.experimental import pallas as pl
from jax.experimental.pallas import tpu as pltpu

F32 = jnp.float32
BF16 = jnp.bfloat16
EPS = 1e-6

D_MODEL = 2048
D_FF = 5632
GDN_QK_HEADS = 16
GDN_V_HEADS = 32
GDN_HEAD = 128
GDN_CONV = 4
GDN_CHUNK = 64
GDN_KEY_DIM = GDN_QK_HEADS * GDN_HEAD
GDN_VAL_DIM = GDN_V_HEADS * GDN_HEAD
GDN_CONV_DIM = 2 * GDN_KEY_DIM + GDN_VAL_DIM
GDN_IN_DIM = GDN_CONV_DIM + GDN_VAL_DIM + 2 * GDN_V_HEADS
DIFF_HEADS = 8
DIFF_HEAD = 128
DIFF_V_HEAD = 2 * DIFF_HEAD
DIFF_QK_DIM = DIFF_HEADS * 2 * DIFF_HEAD
DIFF_V_DIM = DIFF_HEADS * DIFF_V_HEAD
ROPE_DIM = DIFF_HEAD // 4
ROPE_THETA = 500000.0
PAGE_SIZE = 128

V7X_LANES = 128
V7X_VMEM_LIMIT_BYTES = 56 * 1024 * 1024

GDN_SOLVE_WIDTH = 128
GDN_SOLVE_INTERLEAVE = 4
DECODE_PAGES_PER_STEP = 4
FLASH_SUB_ROWS = 256


def _cparams(semantics, vmem_bytes):
    return pltpu.CompilerParams(dimension_semantics=semantics,
                                vmem_limit_bytes=int(min(max(vmem_bytes, 16 * 2**20), V7X_VMEM_LIMIT_BYTES)))


def _dot(a, b):
    return jnp.dot(a, b, preferred_element_type=F32)


def _dot_nt(a, b):
    return lax.dot_general(a, b, (((1,), (1,)), ((), ())), preferred_element_type=F32)


def _dot_tn(a, b):
    return lax.dot_general(a, b, (((0,), (0,)), ((), ())), preferred_element_type=F32)


def _rms_rows(x):
    return x * lax.rsqrt(jnp.mean(x * x, axis=-1, keepdims=True) + EPS)


def _silu(x):
    return x * jax.nn.sigmoid(x)


def _ffn_body(x_ref, g_ref, wg_ref, wu_ref, wd_ref, o_ref, xn_ref):
    j = pl.program_id(1)

    @pl.when(j == 0)
    def _():
        x = x_ref[...]
        xn_ref[...] = (_rms_rows(x) * g_ref[...]).astype(BF16)
        o_ref[...] = x

    xn = xn_ref[...]
    gate = _dot(xn, wg_ref[...])
    up = _dot(xn, wu_ref[...])
    act = (_silu(gate) * up).astype(BF16)
    o_ref[...] += 0.5 * _dot(act, wd_ref[...])


def _ffn(x, g_all, wgu_all, wdn_all, layer, tm, tf):
    m, d = x.shape
    f = wdn_all.shape[1]
    nf = f // tf
    vmem = 4 * tm * d * 4 + tm * d * 2 + 2 * (2 * d * tf + tf * d) * 2 + 4 * tm * tf * 4
    return pl.pallas_call(
        _ffn_body,
        grid=(m // tm, nf),
        in_specs=[
            pl.BlockSpec((tm, d), lambda i, j: (i, 0)),
            pl.BlockSpec((None, 1, d), lambda i, j: (layer, 0, 0)),
            pl.BlockSpec((None, d, tf), lambda i, j: (layer, 0, j)),
            pl.BlockSpec((None, d, tf), lambda i, j: (layer, 0, nf + j)),
            pl.BlockSpec((None, tf, d), lambda i, j: (layer, j, 0)),
        ],
        out_specs=pl.BlockSpec((tm, d), lambda i, j: (i, 0)),
        out_shape=jax.ShapeDtypeStruct((m, d), F32),
        scratch_shapes=[pltpu.VMEM((tm, d), BF16)],
        compiler_params=_cparams(("parallel", "arbitrary"), vmem),
        name="half_ffn",
    )(x, g_all, wgu_all, wgu_all, wdn_all)


def _proj_body(x_ref, g_ref, w_ref, o_ref, xn_ref):
    @pl.when(pl.program_id(1) == 0)
    def _():
        xn_ref[...] = (_rms_rows(x_ref[...]) * g_ref[...]).astype(BF16)

    o_ref[...] = _dot(xn_ref[...], w_ref[...])


def _proj(x, g_all, w_all, layer, w_layer, tm, tn):
    m, d = x.shape
    n = w_all.shape[2]
    vmem = 2 * tm * d * 4 + tm * d * 2 + 2 * d * tn * 2 + 3 * tm * tn * 4
    return pl.pallas_call(
        _proj_body,
        grid=(m // tm, n // tn),
        in_specs=[
            pl.BlockSpec((tm, d), lambda i, j: (i, 0)),
            pl.BlockSpec((None, 1, d), lambda i, j: (layer, 0, 0)),
            pl.BlockSpec((None, d, tn), lambda i, j: (w_layer, 0, j)),
        ],
        out_specs=pl.BlockSpec((tm, tn), lambda i, j: (i, j)),
        out_shape=jax.ShapeDtypeStruct((m, n), F32),
        scratch_shapes=[pltpu.VMEM((tm, d), BF16)],
        compiler_params=_cparams(("parallel", "arbitrary"), vmem),
        name="norm_proj",
    )(x, g_all, w_all)


def _diffproj_body(x_ref, g_ref, w_ref, qn_ref, kn_ref, cos_ref, sa_ref, sb_ref,
                   q_ref, k_ref, v_ref, xn_ref, *, nq):
    j = pl.program_id(1)

    @pl.when(j == 0)
    def _():
        xn_ref[...] = (_rms_rows(x_ref[...]) * g_ref[...]).astype(BF16)

    acc = _dot(xn_ref[...], w_ref[...])
    heads = acc.shape[1] // DIFF_HEAD

    def norm_rope(dst_ref, wn):
        cos = cos_ref[...]
        sa = sa_ref[...]
        sb = sb_ref[...]
        for h in range(heads):
            y = _rms_rows(acc[:, h * DIFF_HEAD:(h + 1) * DIFF_HEAD]) * wn
            y = (y * cos + pltpu.roll(y, DIFF_HEAD - ROPE_DIM // 2, 1) * sa
                 + pltpu.roll(y, ROPE_DIM // 2, 1) * sb)
            dst_ref[:, h * DIFF_HEAD:(h + 1) * DIFF_HEAD] = y

    @pl.when(j < nq)
    def _():
        norm_rope(q_ref, qn_ref[...])

    @pl.when(jnp.logical_and(j >= nq, j < 2 * nq))
    def _():
        norm_rope(k_ref, kn_ref[...])

    @pl.when(j >= 2 * nq)
    def _():
        v_ref[...] = acc


def _diffproj(x, g_all, w_all, qn_all, kn_all, cos, sa, sb, layer, w_layer, tm, tn):
    m, d = x.shape
    nq = DIFF_QK_DIM // tn
    nv = DIFF_V_DIM // tn
    vmem = 2 * tm * d * 4 + tm * d * 2 + 2 * d * tn * 2 + 8 * tm * tn * 4 + 6 * tm * 128 * 4
    out = jax.ShapeDtypeStruct((m, DIFF_QK_DIM), F32)
    return pl.pallas_call(
        functools.partial(_diffproj_body, nq=nq),
        grid=(m // tm, 2 * nq + nv),
        in_specs=[
            pl.BlockSpec((tm, d), lambda i, j: (i, 0)),
            pl.BlockSpec((None, 1, d), lambda i, j: (layer, 0, 0)),
            pl.BlockSpec((None, d, tn), lambda i, j: (w_layer, 0, j)),
            pl.BlockSpec((None, 1, DIFF_HEAD), lambda i, j: (w_layer, 0, 0)),
            pl.BlockSpec((None, 1, DIFF_HEAD), lambda i, j: (w_layer, 0, 0)),
            pl.BlockSpec((tm, DIFF_HEAD), lambda i, j: (i, 0)),
            pl.BlockSpec((tm, DIFF_HEAD), lambda i, j: (i, 0)),
            pl.BlockSpec((tm, DIFF_HEAD), lambda i, j: (i, 0)),
        ],
        out_specs=[
            pl.BlockSpec((tm, tn), lambda i, j: (i, jnp.minimum(j, nq - 1))),
            pl.BlockSpec((tm, tn), lambda i, j: (i, jnp.clip(j - nq, 0, nq - 1))),
            pl.BlockSpec((tm, tn), lambda i, j: (i, jnp.clip(j - 2 * nq, 0, nv - 1))),
        ],
        out_shape=[out, out, jax.ShapeDtypeStruct((m, DIFF_V_DIM), F32)],
        scratch_shapes=[pltpu.VMEM((tm, d), BF16)],
        compiler_params=_cparams(("parallel", "arbitrary"), vmem),
        name="diff_proj",
    )(x, g_all, w_all, qn_all, kn_all, cos, sa, sb)


def _outproj_body(y_ref, w_ref, x_ref, o_ref):
    o_ref[...] = x_ref[...] + _dot(y_ref[...], w_ref[...])


def _outproj(y, w_all, x, w_layer, tm, tn):
    m, k = y.shape
    d = x.shape[1]
    vmem = 2 * tm * k * 2 + 2 * k * tn * 2 + 5 * tm * tn * 4
    return pl.pallas_call(
        _outproj_body,
        grid=(m // tm, d // tn),
        in_specs=[
            pl.BlockSpec((tm, k), lambda i, j: (i, 0)),
            pl.BlockSpec((None, k, tn), lambda i, j: (w_layer, 0, j)),
            pl.BlockSpec((tm, tn), lambda i, j: (i, j)),
        ],
        out_specs=pl.BlockSpec((tm, tn), lambda i, j: (i, j)),
        out_shape=jax.ShapeDtypeStruct((m, d), F32),
        compiler_params=_cparams(("parallel", "arbitrary"), vmem),
        name="out_proj",
    )(y, w_all, x)


def _unit_lower_inverse_minus_identity(mms, level_masks, base_mask):
    ys = [-(mm * base_mask) for mm in mms]
    for mask in level_masks:
        os_ = [mm * mask for mm in mms]
        y_bf = [y.astype(BF16) for y in ys]
        zs = [o + _dot(yb, o.astype(BF16)) for o, yb in zip(os_, y_bf)]
        ys = [y - z - _dot(z.astype(BF16), yb) for y, z, yb in zip(ys, zs, y_bf)]
    return ys


def _gdn_body(qp_ref, kp_ref, vp_ref, z_ref, q8_ref, k8_ref, v8_ref, cwq_ref, cwk_ref, cwv_ref,
              b_ref, a_ref, alog_ref, dtb_ref, nw_ref, s0_ref,
              o_ref, sout_ref,
              q_s, k_s, v_s, u_s, w_s, at_s, grow_s, gcc_s, bc_s,
              *, t_len, t_valid, gw, ilv):
    C = GDN_CHUNK
    HD = GDN_HEAD
    nc = t_len // C
    ng = t_len // gw
    cpg = gw // C
    heads = (0, 1)
    rt = min(256, t_len)
    n_rt = t_len // rt

    def conv_tile(x_ref, p8_ref, cw_ref, r):
        head = p8_ref[...] if r == 0 else x_ref[r * rt - 8:r * rt, :]
        cur = x_ref[r * rt:(r + 1) * rt, :]
        ext = jnp.concatenate([head, cur], axis=0)
        w = cw_ref[...]
        y = ext[5:5 + rt, :] * w[0:1, :]
        y = y + ext[6:6 + rt, :] * w[1:2, :]
        y = y + ext[7:7 + rt, :] * w[2:3, :]
        y = y + cur * w[3:4, :]
        y = _silu(y)
        if t_valid < t_len:
            rows = r * rt + lax.broadcasted_iota(jnp.int32, y.shape, 0)
            y = jnp.where(rows < t_valid, y, 0.0)
        return y

    def l2n(x):
        return x * lax.rsqrt(jnp.sum(x * x, axis=-1, keepdims=True) + EPS)

    for r in range(n_rt):
        rows = slice(r * rt, (r + 1) * rt)
        q_s[rows, :] = l2n(conv_tile(qp_ref, q8_ref, cwq_ref, r)) * (HD ** -0.5)
        k_s[rows, :] = l2n(conv_tile(kp_ref, k8_ref, cwk_ref, r))
        v_s[rows, :] = conv_tile(vp_ref, v8_ref, cwv_ref, r)

    ri = lax.broadcasted_iota(jnp.int32, (gw, gw), 0)
    ci = lax.broadcasted_iota(jnp.int32, (gw, gw), 1)
    chunk_shift = C.bit_length() - 1
    same_chunk = (ri >> chunk_shift) == (ci >> chunk_shift)
    lower = jnp.logical_and(same_chunk, ci <= ri)
    strict = jnp.logical_and(same_chunk, ci < ri)
    upper_ones = jnp.where(jnp.logical_and(same_chunk, ri <= ci), 1.0, 0.0)
    eye = jnp.where(ri == ci, 1.0, 0.0)
    hi = lax.Precision.HIGHEST
    for h in heads:
        beta = jax.nn.sigmoid(b_ref[h])
        g = -jnp.exp(alog_ref[h]) * jax.nn.softplus(a_ref[h] + dtb_ref[h])
        if t_valid < t_len:
            pos = (lax.broadcasted_iota(jnp.int32, g.shape, 0) * gw
                   + lax.broadcasted_iota(jnp.int32, g.shape, 1))
            g = jnp.where(pos < t_valid, g, 0.0)
            beta = jnp.where(pos < t_valid, beta, 0.0)
        gc = jnp.dot(g, upper_ones, precision=hi, preferred_element_type=F32)
        grow_s[h] = gc
        gc_t = lax.dot_general(eye, gc, (((1,), (1,)), ((), ())), precision=hi,
                               preferred_element_type=F32)
        beta_t = lax.dot_general(eye, beta, (((1,), (1,)), ((), ())), precision=hi,
                                 preferred_element_type=F32)
        for i in range(ng):
            gcc_s[h, i * gw:(i + 1) * gw, :] = jnp.broadcast_to(gc_t[:, i:i + 1], (gw, HD))
            bc_s[h, i * gw:(i + 1) * gw, :] = jnp.broadcast_to(beta_t[:, i:i + 1], (gw, HD))

    base_mask = jnp.where((ri >> 1) == (ci >> 1), 1.0, 0.0)
    level_masks = []
    for s in range(1, chunk_shift):
        same_big = (ri >> (s + 1)) == (ci >> (s + 1))
        same_small = (ri >> s) == (ci >> s)
        level_masks.append(jnp.where(same_big, 1.0, 0.0) - jnp.where(same_small, 1.0, 0.0))

    def lane_tile(x):
        if gw <= HD:
            return x[:, :gw]
        return jnp.concatenate([x] * (gw // HD), axis=1)

    def solve_groups(it, carry):
        chains = []
        for gg in range(ilv):
            gi = it * ilv + gg
            r0 = pl.multiple_of(gi * gw, gw)
            kc = k_s[pl.ds(r0, gw), :]
            k_bf = kc.astype(BF16)
            qk = _dot_nt(q_s[pl.ds(r0, gw), :].astype(BF16), k_bf)
            for h in heads:
                gcc = gcc_s[h, pl.ds(r0, gw), :]
                gd = lane_tile(gcc) - grow_s[h, pl.ds(gi, 1), :]
                dec = jnp.where(lower, jnp.exp(jnp.where(lower, gd, 0.0)), 0.0)
                beta_c = bc_s[h, pl.ds(r0, gw), :]
                chains.append((h, r0, kc * beta_c, k_bf, dec, gcc, beta_c, qk))
        mms = [jnp.where(strict, _dot_nt(kb.astype(BF16), k_bf) * dec, 0.0)
               for (_, _, kb, k_bf, dec, _, _, _) in chains]
        ys = _unit_lower_inverse_minus_identity(mms, level_masks, base_mask)
        rhss = [jnp.concatenate([v_s[pl.ds(r0, gw), h * HD:(h + 1) * HD] * beta_c, kb * jnp.exp(gcc)], axis=1)
                for (h, r0, kb, _, _, gcc, beta_c, _) in chains]
        uws = [rhs + _dot(y.astype(BF16), rhs.astype(BF16)) for rhs, y in zip(rhss, ys)]
        for (h, r0, _, _, dec, _, _, qk), uw in zip(chains, uws):
            u_s[h, pl.ds(r0, gw), :] = uw[:, :HD]
            w_s[h, pl.ds(r0, gw), :] = uw[:, HD:].astype(BF16)
            at = qk * dec
            for k in range(cpg):
                at_s[h, pl.ds(pl.multiple_of(r0 + k * C, C), C), :] = (
                    at[k * C:(k + 1) * C, k * C:(k + 1) * C].astype(BF16))
        return carry

    lax.fori_loop(0, ng // ilv, solve_groups, 0)

    nw = nw_ref[...]

    def step(c, states):
        r0 = pl.multiple_of(c * C, C)
        kc = k_s[pl.ds(r0, C), :]
        qc = q_s[pl.ds(r0, C), :]
        gccs = [gcc_s[h, pl.ds(r0, C), :] for h in heads]
        lhss = [jnp.concatenate([w_s[h, pl.ds(r0, C), :], (qc * jnp.exp(gccs[h])).astype(BF16)], axis=0)
                for h in heads]
        wss = [_dot(lhss[h], states[h].astype(BF16)) for h in heads]
        vns = [(u_s[h, pl.ds(r0, C), :] - wss[h][:C]).astype(BF16) for h in heads]
        glast = [gccs[h][C - 1:C, :] for h in heads]
        kes = [(kc * jnp.exp(glast[h] - gccs[h])).astype(BF16) for h in heads]
        outs = [wss[h][C:] + _dot(at_s[h, pl.ds(r0, C), :], vns[h]) for h in heads]
        new_states = [states[h] * jnp.exp(glast[h]) + _dot_tn(kes[h], vns[h]) for h in heads]
        for h in heads:
            zc = z_ref[pl.ds(r0, C), h * HD:(h + 1) * HD]
            o_ref[pl.ds(r0, C), h * HD:(h + 1) * HD] = (_rms_rows(outs[h]) * nw * _silu(zc)).astype(BF16)
        return tuple(new_states)

    s_fin = lax.fori_loop(0, nc, step, (s0_ref[0], s0_ref[1]))
    sout_ref[0] = s_fin[0]
    sout_ref[1] = s_fin[1]


def _gdn_core(p, prev8, conv_w_all, b_rows, a_rows, alog_b, dtb_b, norm_w_all, s0, layer,
              batch, t_len, t_valid):
    C = GDN_CHUNK
    HD = GDN_HEAD
    ng8, gw = b_rows.shape[2], b_rows.shape[3]
    ng = t_len // gw
    alog_b = jnp.broadcast_to(alog_b[:, :, None, None], alog_b.shape + (1, gw))
    dtb_b = jnp.broadcast_to(dtb_b[:, :, None, None], dtb_b.shape + (1, gw))
    ilv = math.gcd(ng, GDN_SOLVE_INTERLEAVE)
    kq = GDN_KEY_DIM // HD
    vmem = (2 * 6 * t_len * HD * 4 + 2 * t_len * 2 * HD * 2 + (4 + 2) * t_len * HD * 4
            + 2 * t_len * HD * 2 * 2 + 2 * 2 * t_len * HD * 4 + 12 * 2**20)
    grid = (batch, GDN_QK_HEADS)
    return pl.pallas_call(
        functools.partial(_gdn_body, t_len=t_len, t_valid=t_valid, gw=gw, ilv=ilv),
        grid=grid,
        in_specs=[
            pl.BlockSpec((t_len, HD), lambda b, h: (b, h)),
            pl.BlockSpec((t_len, HD), lambda b, h: (b, kq + h)),
            pl.BlockSpec((t_len, 2 * HD), lambda b, h: (b, kq + h)),
            pl.BlockSpec((t_len, 2 * HD), lambda b, h: (b, kq + GDN_V_HEADS // 2 + h)),
            pl.BlockSpec((None, 8, HD), lambda b, h: (b, 0, h)),
            pl.BlockSpec((None, 8, HD), lambda b, h: (b, 0, kq + h)),
            pl.BlockSpec((None, 8, 2 * HD), lambda b, h: (b, 0, kq + h)),
            pl.BlockSpec((None, GDN_CONV, HD), lambda b, h: (layer, 0, h)),
            pl.BlockSpec((None, GDN_CONV, HD), lambda b, h: (layer, 0, kq + h)),
            pl.BlockSpec((None, GDN_CONV, 2 * HD), lambda b, h: (layer, 0, kq + h)),
            pl.BlockSpec((None, 2, ng8, gw), lambda b, h: (b, h, 0, 0)),
            pl.BlockSpec((None, 2, ng8, gw), lambda b, h: (b, h, 0, 0)),
            pl.BlockSpec((None, 2, 1, gw), lambda b, h: (layer, h, 0, 0)),
            pl.BlockSpec((None, 2, 1, gw), lambda b, h: (layer, h, 0, 0)),
            pl.BlockSpec((None, 1, HD), lambda b, h: (layer, 0, 0)),
            pl.BlockSpec((None, 2, HD, HD), lambda b, h: (b, h, 0, 0)),
        ],
        out_specs=[
            pl.BlockSpec((t_len, 2 * HD), lambda b, h: (b, h)),
            pl.BlockSpec((None, 2, HD, HD), lambda b, h: (b, h, 0, 0)),
        ],
        out_shape=[
            jax.ShapeDtypeStruct((batch * t_len, GDN_VAL_DIM), BF16),
            jax.ShapeDtypeStruct((batch, GDN_V_HEADS, HD, HD), F32),
        ],
        scratch_shapes=[
            pltpu.VMEM((t_len, HD), F32),
            pltpu.VMEM((t_len, HD), F32),
            pltpu.VMEM((t_len, 2 * HD), F32),
            pltpu.VMEM((2, t_len, HD), F32),
            pltpu.VMEM((2, t_len, HD), BF16),
            pltpu.VMEM((2, t_len, C), BF16),
            pltpu.VMEM((2, ng8, gw), F32),
            pltpu.VMEM((2, t_len, HD), F32),
            pltpu.VMEM((2, t_len, HD), F32),
        ],
        compiler_params=_cparams(("parallel", "arbitrary"), vmem),
        name="gdn_core",
    )(p, p, p, p, prev8, prev8, prev8, conv_w_all, conv_w_all, conv_w_all,
      b_rows, a_rows, alog_b, dtb_b, norm_w_all, s0)


def _diff_lambda(lq1_ref, lk1_ref, lq2_ref, lk2_ref, lam_init):
    s1 = jnp.sum(lq1_ref[...] * lk1_ref[...], axis=-1, keepdims=True)
    s2 = jnp.sum(lq2_ref[...] * lk2_ref[...], axis=-1, keepdims=True)
    return jnp.exp(s1) - jnp.exp(s2) + lam_init


def _flash_body(q_ref, k_ref, v_ref, lq1_ref, lk1_ref, lq2_ref, lk2_ref, sn_ref, o_ref,
                m_s, l_s, acc_s, *, tq, sub, lam_init):
    qi = pl.program_id(2)
    scale = DIFF_HEAD ** -0.5
    q = q_ref[...]
    q_maps = (q[:, :DIFF_HEAD].astype(BF16), q[:, DIFF_HEAD:].astype(BF16))
    m_s[...] = jnp.full(m_s.shape, -jnp.inf, F32)
    l_s[...] = jnp.zeros(l_s.shape, F32)
    acc_s[...] = jnp.zeros(acc_s.shape, F32)
    c2 = scale * math.log2(math.e)
    maps = (0, 1)

    def block(kb, masked):
        r0 = pl.multiple_of(kb * tq, tq)
        k = k_ref[pl.ds(r0, tq), :]
        v = v_ref[pl.ds(r0, tq), :].astype(BF16)
        ks = [k[:, c * DIFF_HEAD:(c + 1) * DIFF_HEAD].astype(BF16) for c in maps]
        chains = [(r, c) for r in range(tq // sub) for c in maps]

        def scores(r, c):
            return _dot_nt(q_maps[c][r * sub:(r + 1) * sub, :], ks[c])

        s_next = scores(*chains[0])
        for i, (r, c) in enumerate(chains):
            s = s_next
            if i + 1 < len(chains):
                s_next = scores(*chains[i + 1])
            rs = slice(r * sub, (r + 1) * sub)
            if masked:
                keep = (lax.broadcasted_iota(jnp.int32, (sub, tq), 1)
                        <= lax.broadcasted_iota(jnp.int32, (sub, tq), 0) + r * sub)
                s = jnp.where(keep, s, -jnp.inf)
            m_prev = m_s[c, rs, :]
            m_new = jnp.maximum(m_prev, jnp.max(s, axis=-1, keepdims=True))
            p = jnp.exp2((s - jnp.concatenate([m_new] * (tq // V7X_LANES), axis=1)) * c2)
            alpha = jnp.exp2((m_prev - m_new) * c2)
            p_lanes = p[:, :V7X_LANES]
            for t in range(1, tq // V7X_LANES):
                p_lanes = p_lanes + p[:, t * V7X_LANES:(t + 1) * V7X_LANES]
            l_s[c, rs, :] = alpha * l_s[c, rs, :] + p_lanes
            acc_s[c, rs, :] = (jnp.concatenate([alpha] * (DIFF_V_HEAD // V7X_LANES), axis=1) * acc_s[c, rs, :]
                               + _dot(p.astype(BF16), v))
            m_s[c, rs, :] = m_new

    def body(kb, carry):
        block(kb, False)
        return carry

    lax.fori_loop(0, qi, body, 0)
    block(qi, True)
    lam = _diff_lambda(lq1_ref, lk1_ref, lq2_ref, lk2_ref, lam_init)
    l0 = jnp.sum(l_s[0], axis=-1, keepdims=True)
    l1 = jnp.sum(l_s[1], axis=-1, keepdims=True)
    o = acc_s[0] / l0 - lam * (acc_s[1] / l1)
    o_ref[...] = (_rms_rows(o) * sn_ref[...] * (1.0 - lam_init)).astype(BF16)


def _flash(q, k, v, lq1, lk1, lq2, lk2, sn_all, w_layer, lam_init, batch, t_len, tq):
    nq = t_len // tq
    dv = DIFF_V_HEAD
    vmem = 2 * tq * dv * 4 + 4 * t_len * dv * 4 + 2 * tq * dv * 2 + 2 * tq * dv * 4 + 8 * tq * tq * 4 + 4 * 2**20
    lam_spec = pl.BlockSpec((None, 1, DIFF_HEAD), lambda b, h, i: (w_layer, 0, 0))
    return pl.pallas_call(
        functools.partial(_flash_body, tq=tq, sub=min(FLASH_SUB_ROWS, tq), lam_init=lam_init),
        grid=(batch, DIFF_HEADS, nq),
        in_specs=[
            pl.BlockSpec((tq, dv), lambda b, h, i: (b * nq + i, h)),
            pl.BlockSpec((t_len, dv), lambda b, h, i: (b, h)),
            pl.BlockSpec((t_len, dv), lambda b, h, i: (b, h)),
            lam_spec, lam_spec, lam_spec, lam_spec,
            pl.BlockSpec((None, 1, dv), lambda b, h, i: (w_layer, 0, 0)),
        ],
        out_specs=pl.BlockSpec((tq, dv), lambda b, h, i: (b * nq + i, h)),
        out_shape=jax.ShapeDtypeStruct((batch * t_len, DIFF_V_DIM), BF16),
        scratch_shapes=[
            pltpu.VMEM((2, tq, V7X_LANES), F32),
            pltpu.VMEM((2, tq, V7X_LANES), F32),
            pltpu.VMEM((2, tq, dv), F32),
        ],
        compiler_params=_cparams(("parallel", "parallel", "arbitrary"), vmem),
        name="diff_flash",
    )(q, k, v, lq1, lk1, lq2, lk2, sn_all)


def _decode_body(pt_ref, q_ref, kn_ref, vn_ref, *rest, n_steps, pps, n_new, lam_init):
    del pt_ref
    kc_refs, vc_refs = rest[:pps], rest[pps:2 * pps]
    lq1_ref, lk1_ref, lq2_ref, lk2_ref, sn_ref, o_ref, m_s, l_s, acc_s = rest[2 * pps:]
    p = pl.program_id(1)
    scale = DIFF_HEAD ** -0.5
    n_half = q_ref.shape[1]
    n_rows = 2 * n_half
    n_cols = vn_ref.shape[0]

    @pl.when(p == 0)
    def _():
        m_s[...] = jnp.full(m_s.shape, -jnp.inf, F32)
        l_s[...] = jnp.zeros(l_s.shape, F32)
        acc_s[...] = jnp.zeros(acc_s.shape, F32)

    row_head = (lax.broadcasted_iota(jnp.int32, (n_rows, n_cols), 0) % n_half) // n_new
    col = lax.broadcasted_iota(jnp.int32, (n_rows, n_cols), 1)
    head_ok = row_head == col % DIFF_HEADS

    q_bf = [q_ref[c].astype(BF16) for c in range(2)]

    def update(k_refs, v_refs, mask):
        ss = [jnp.concatenate(
            [_dot_nt(q_bf[c], k_ref[pl.ds(c, n_cols, stride=2), :].astype(BF16)) for c in range(2)],
            axis=0) * scale for k_ref in k_refs]
        ss = [jnp.where(mask, s, -jnp.inf) for s in ss]
        m_prev = m_s[...]
        m_new = m_prev
        for s in ss:
            m_new = jnp.maximum(m_new, jnp.max(s, axis=-1, keepdims=True))
        alpha = jnp.exp(m_prev - m_new)
        prs = [jnp.exp(s - m_new) for s in ss]
        l_new = alpha * l_s[...]
        for pr in prs:
            l_new = l_new + jnp.sum(pr, axis=-1, keepdims=True)
        pvs = [_dot(pr.astype(BF16), v_ref[...].astype(BF16)) for pr, v_ref in zip(prs, v_refs)]
        acc = alpha * acc_s[...]
        for pv in pvs:
            acc = acc + pv
        l_s[...] = l_new
        m_s[...] = m_new
        acc_s[...] = acc

    update(kc_refs, vc_refs, head_ok)

    @pl.when(p == n_steps - 1)
    def _():
        qidx = lax.broadcasted_iota(jnp.int32, (n_rows, n_cols), 0) % n_new
        causal = col // DIFF_HEADS <= qidx
        update([kn_ref], [vn_ref], jnp.logical_and(head_ok, causal))
        lam = _diff_lambda(lq1_ref, lk1_ref, lq2_ref, lk2_ref, lam_init)
        o = acc_s[...] / l_s[...]
        od = o[:n_half] - lam * o[n_half:]
        o_ref[...] = _rms_rows(od) * sn_ref[...] * (1.0 - lam_init)


def _decode(page_table, q_maps, kn_pad, vn_pad, cache_k, cache_v, lq1, lk1, lq2, lk2, sn_all,
            w_layer, lam_init, n_new):
    bs, n_pages = page_table.shape
    n_half = q_maps.shape[2]
    k_rows, v_rows = cache_k.shape[2], cache_v.shape[2]
    pps = math.gcd(n_pages, DECODE_PAGES_PER_STEP)
    n_steps = n_pages // pps
    lam_spec = pl.BlockSpec((None, 1, DIFF_HEAD), lambda b, p, pt: (w_layer, 0, 0))
    vmem = 3 * pps * (k_rows * DIFF_HEAD + v_rows * DIFF_V_HEAD) * 4 + 20 * 2**20

    def page_spec(rows, width, i):
        return pl.BlockSpec((None, None, rows, width), lambda b, p, pt: (w_layer, pt[b, p * pps + i], 0, 0))

    grid_spec = pltpu.PrefetchScalarGridSpec(
        num_scalar_prefetch=1,
        grid=(bs, n_steps),
        in_specs=[
            pl.BlockSpec((None, 2, n_half, DIFF_HEAD), lambda b, p, pt: (b, 0, 0, 0)),
            pl.BlockSpec((None, k_rows, DIFF_HEAD), lambda b, p, pt: (b, 0, 0)),
            pl.BlockSpec((None, v_rows, DIFF_V_HEAD), lambda b, p, pt: (b, 0, 0)),
            *[page_spec(k_rows, DIFF_HEAD, i) for i in range(pps)],
            *[page_spec(v_rows, DIFF_V_HEAD, i) for i in range(pps)],
            lam_spec, lam_spec, lam_spec, lam_spec,
            pl.BlockSpec((None, 1, DIFF_V_HEAD), lambda b, p, pt: (w_layer, 0, 0)),
        ],
        out_specs=pl.BlockSpec((None, n_half, DIFF_V_HEAD), lambda b, p, pt: (b, 0, 0)),
        scratch_shapes=[
            pltpu.VMEM((2 * n_half, 1), F32),
            pltpu.VMEM((2 * n_half, 1), F32),
            pltpu.VMEM((2 * n_half, DIFF_V_HEAD), F32),
        ],
    )
    return pl.pallas_call(
        functools.partial(_decode_body, n_steps=n_steps, pps=pps, n_new=n_new, lam_init=lam_init),
        grid_spec=grid_spec,
        out_shape=jax.ShapeDtypeStruct((bs, n_half, DIFF_V_HEAD), F32),
        compiler_params=_cparams(("parallel", "arbitrary"), vmem),
        name="diff_decode",
    )(page_table, q_maps, kn_pad, vn_pad, *([cache_k] * pps), *([cache_v] * pps), lq1, lk1, lq2, lk2, sn_all)


def _rope_tables(pos):
    half = ROPE_DIM // 2
    inv = ROPE_THETA ** (-jnp.arange(half, dtype=F32) * 2.0 / ROPE_DIM)
    ang = pos.astype(F32)[:, None] * inv[None, :]
    cos, sin = jnp.cos(ang), jnp.sin(ang)
    n = pos.shape[0]
    tail = DIFF_HEAD - ROPE_DIM
    c = jnp.concatenate([cos, cos, jnp.ones((n, tail), F32)], axis=1)
    sa = jnp.concatenate([-sin, jnp.zeros((n, half + tail), F32)], axis=1)
    sb = jnp.concatenate([jnp.zeros((n, half), F32), sin, jnp.zeros((n, tail), F32)], axis=1)
    return c, sa, sb


def _gate_rows(p, batch, t_len):
    gw = min(GDN_SOLVE_WIDTH, t_len)
    ng = t_len // gw
    ba = p[:, GDN_CONV_DIM + GDN_VAL_DIM:GDN_CONV_DIM + GDN_VAL_DIM + 2 * GDN_V_HEADS]
    ba = ba.reshape(batch, ng, gw, 2, GDN_V_HEADS).transpose(3, 0, 4, 1, 2)
    ng8 = -(-ng // 8) * 8
    ba = jnp.pad(ba, ((0, 0), (0, 0), (0, 0), (0, ng8 - ng), (0, 0)))
    return ba[0], ba[1]


def kernel(x_prompt, x_sample, state_delta, state_conv, cache_k, cache_v, page_table, norm_ffn1, ffn1_w_gu, ffn1_w_dn, norm_mix, norm_ffn2, ffn2_w_gu, ffn2_w_dn, gdn_w_in, gdn_conv_w, gdn_a_log, gdn_dt_bias, gdn_norm_w, gdn_w_out, diff_w_in, diff_q_norm, diff_k_norm, diff_lam_q1, diff_lam_k1, diff_lam_q2, diff_lam_k2, diff_sub_norm, diff_w_out):
    bp, tp, d = x_prompt.shape
    bs, ts, _ = x_sample.shape
    depth = norm_ffn1.shape[0]
    n_pages = page_table.shape[1]
    past = n_pages * PAGE_SIZE
    C = GDN_CHUNK

    bf = lambda w: w.astype(BF16)
    row = lambda w: w.reshape(w.shape[0], 1, w.shape[1])
    w_gu1, w_dn1, w_gu2, w_dn2 = bf(ffn1_w_gu), bf(ffn1_w_dn), bf(ffn2_w_gu), bf(ffn2_w_dn)
    tn = 512
    gdn_n = -(-GDN_IN_DIM // tn) * tn
    w_gin = jnp.pad(bf(gdn_w_in), ((0, 0), (0, 0), (0, gdn_n - GDN_IN_DIM)))
    w_gout, w_din, w_dout = bf(gdn_w_out), bf(diff_w_in), bf(diff_w_out)
    g_ffn1, g_mix, g_ffn2 = row(norm_ffn1), row(norm_mix), row(norm_ffn2)
    alog_b, dtb_b = gdn_a_log, gdn_dt_bias
    gdn_nw = row(gdn_norm_w)
    qn, kn_w = row(diff_q_norm), row(diff_k_norm)
    lq1, lk1, lq2, lk2 = row(diff_lam_q1), row(diff_lam_k1), row(diff_lam_q2), row(diff_lam_k2)
    sub_n = row(diff_sub_norm)
    ck = cache_k.reshape(cache_k.shape[0], cache_k.shape[1], PAGE_SIZE * DIFF_HEADS * 2, DIFF_HEAD)
    cv = cache_v.reshape(cache_v.shape[0], cache_v.shape[1], PAGE_SIZE * DIFF_HEADS, DIFF_V_HEAD)

    rope_p = _rope_tables(jnp.arange(tp))
    rope_p = tuple(jnp.tile(t, (bp, 1)) for t in rope_p)
    rope_s = _rope_tables(past + jnp.arange(ts))
    rope_s = tuple(jnp.tile(t, (bs, 1)) for t in rope_s)

    xp = x_prompt.reshape(bp * tp, d)
    xs = x_sample.reshape(bs * ts, d)
    mp, ms = bp * tp, bs * ts
    tm_p, tm_s = 512, ms
    tf = 512
    ts_pad = -(-ts // C) * C

    prev8_p = jnp.zeros((bp, 8, GDN_CONV_DIM), F32)
    s0_p = jnp.zeros((bp, GDN_V_HEADS, GDN_HEAD, GDN_HEAD), F32)

    p_delta, p_conv, p_k, p_v = [], [], [], []
    s_delta, s_conv, s_k, s_v = [], [], [], []
    for i in range(depth):
        xp = _ffn(xp, g_ffn1, w_gu1, w_dn1, i, tm_p, tf)
        xs = _ffn(xs, g_ffn1, w_gu1, w_dn1, i, tm_s, tf)
        j = i // 2
        if i % 2 == 0:
            pp = _proj(xp, g_mix, w_gin, i, j, tm_p, tn)
            ps = _proj(xs, g_mix, w_gin, i, j, tm_s, tn)
            b_rows, a_rows = _gate_rows(pp, bp, tp)
            op, sp = _gdn_core(pp, prev8_p, gdn_conv_w, b_rows, a_rows, alog_b, dtb_b, gdn_nw, s0_p, j,
                               bp, tp, tp)
            p_conv.append(pp.reshape(bp, tp, gdn_n)[:, tp - (GDN_CONV - 1):, :GDN_CONV_DIM])
            p_delta.append(sp)
            xp = _outproj(op, w_gout, xp, j, tm_p, tn)

            ps3 = ps.reshape(bs, ts, gdn_n)
            ps_pad = jnp.pad(ps3, ((0, 0), (0, ts_pad - ts), (0, 0))).reshape(bs * ts_pad, gdn_n)
            prev8_s = jnp.pad(state_conv[j], ((0, 0), (8 - (GDN_CONV - 1), 0), (0, 0)))
            b_rows, a_rows = _gate_rows(ps_pad, bs, ts_pad)
            os_, ss = _gdn_core(ps_pad, prev8_s, gdn_conv_w, b_rows, a_rows, alog_b, dtb_b, gdn_nw,
                                state_delta[j], j, bs, ts_pad, ts)
            xp_conv = jnp.concatenate([state_conv[j], ps3[:, :, :GDN_CONV_DIM]], axis=1)
            s_conv.append(xp_conv[:, ts:])
            s_delta.append(ss)
            os_ = os_.reshape(bs, ts_pad, GDN_VAL_DIM)[:, :ts].reshape(ms, GDN_VAL_DIM)
            xs = _outproj(os_, w_gout, xs, j, tm_s, tn)
        else:
            lam_init = 0.8 - 0.6 * math.exp(-0.3 * i)
            qp, kp, vp = _diffproj(xp, g_mix, w_din, qn, kn_w, *rope_p, i, j, tm_p, tn)
            qs, ks, vs = _diffproj(xs, g_mix, w_din, qn, kn_w, *rope_s, i, j, tm_s, tn)
            op = _flash(qp, kp, vp, lq1, lk1, lq2, lk2, sub_n, j, lam_init, bp, tp, 512)
            xp = _outproj(op, w_dout, xp, j, tm_p, tn)
            p_k.append(kp.reshape(bp, tp, DIFF_HEADS, 2, DIFF_HEAD))
            p_v.append(vp.reshape(bp, tp, DIFF_HEADS, DIFF_V_HEAD))

            q_maps = qs.reshape(bs, ts, DIFF_HEADS, 2, DIFF_HEAD).transpose(0, 3, 2, 1, 4)
            q_maps = q_maps.reshape(bs, 2, DIFF_HEADS * ts, DIFF_HEAD)
            kn_pad = jnp.pad(ks.reshape(bs, ts * DIFF_HEADS * 2, DIFF_HEAD),
                             ((0, 0), (0, (PAGE_SIZE - ts) * DIFF_HEADS * 2), (0, 0)))
            vn_pad = jnp.pad(vs.reshape(bs, ts * DIFF_HEADS, DIFF_V_HEAD),
                             ((0, 0), (0, (PAGE_SIZE - ts) * DIFF_HEADS), (0, 0)))
            od = _decode(page_table, q_maps, kn_pad, vn_pad, ck, cv, lq1, lk1, lq2, lk2, sub_n,
                         j, lam_init, ts)
            od = od.reshape(bs, DIFF_HEADS, ts, DIFF_V_HEAD).transpose(0, 2, 1, 3).reshape(ms, DIFF_V_DIM)
            xs = _outproj(od.astype(BF16), w_dout, xs, j, tm_s, tn)
            s_k.append(ks.reshape(bs, ts, DIFF_HEADS, 2, DIFF_HEAD))
            s_v.append(vs.reshape(bs, ts, DIFF_HEADS, DIFF_V_HEAD))
        xp = _ffn(xp, g_ffn2, w_gu2, w_dn2, i, tm_p, tf)
        xs = _ffn(xs, g_ffn2, w_gu2, w_dn2, i, tm_s, tf)

    return (xp.reshape(bp, tp, d), xs.reshape(bs, ts, d),
            jnp.stack(p_delta), jnp.stack(p_conv), jnp.stack(p_k), jnp.stack(p_v),
            jnp.stack(s_delta), jnp.stack(s_conv), jnp.stack(s_k), jnp.stack(s_v))
```

```python
import functools
import math

import jax
import jax.numpy as jnp
from jax import lax
from jax.experimental import pallas as pl
from jax.experimental.pallas import tpu as pltpu

F32 = jnp.float32
BF16 = jnp.bfloat16
EPS = 1e-6

D_MODEL = 2048
D_FF = 5632
GDN_QK_HEADS = 16
GDN_V_HEADS = 32
GDN_HEAD = 128
GDN_CONV = 4
GDN_CHUNK = 64
GDN_KEY_DIM = GDN_QK_HEADS * GDN_HEAD
GDN_VAL_DIM = GDN_V_HEADS * GDN_HEAD
GDN_CONV_DIM = 2 * GDN_KEY_DIM + GDN_VAL_DIM
GDN_IN_DIM = GDN_CONV_DIM + GDN_VAL_DIM + 2 * GDN_V_HEADS
DIFF_HEADS = 8
DIFF_HEAD = 128
DIFF_V_HEAD = 2 * DIFF_HEAD
DIFF_QK_DIM = DIFF_HEADS * 2 * DIFF_HEAD
DIFF_V_DIM = DIFF_HEADS * DIFF_V_HEAD
ROPE_DIM = DIFF_HEAD // 4
ROPE_THETA = 500000.0
PAGE_SIZE = 128

V7X_LANES = 128
V7X_VMEM_LIMIT_BYTES = 56 * 1024 * 1024

GDN_SOLVE_WIDTH = 128
GDN_SOLVE_INTERLEAVE = 4
GDN_MAX_HEADS_PER_STEP = 4
GDN_STEP_VMEM_BYTES = 40 * 1024 * 1024
DECODE_PAGES_PER_STEP = 4
FLASH_SUB_ROWS = 256


def _cparams(semantics, vmem_bytes):
    return pltpu.CompilerParams(dimension_semantics=semantics,
                                vmem_limit_bytes=int(min(max(vmem_bytes, 16 * 2**20), V7X_VMEM_LIMIT_BYTES)))


def _dot(a, b):
    return jnp.dot(a, b, preferred_element_type=F32)


def _dot_nt(a, b):
    return lax.dot_general(a, b, (((1,), (1,)), ((), ())), preferred_element_type=F32)


def _dot_tn(a, b):
    return lax.dot_general(a, b, (((0,), (0,)), ((), ())), preferred_element_type=F32)


def _rms_rows(x):
    return x * lax.rsqrt(jnp.mean(x * x, axis=-1, keepdims=True) + EPS)


def _silu(x):
    return x * jax.nn.sigmoid(x)


def _lane_sumsq(x):
    ones = jnp.ones((x.shape[1], V7X_LANES), BF16)
    return _dot((x * x).astype(BF16), ones)


def _ffn_body(x_ref, g_ref, wg_ref, wu_ref, wd_ref, o_ref, xn_ref):
    j = pl.program_id(1)

    @pl.when(j == 0)
    def _():
        x = x_ref[...]
        xn_ref[...] = (_rms_rows(x) * g_ref[...]).astype(BF16)
        o_ref[...] = x

    xn = xn_ref[...]
    gate = _dot(xn, wg_ref[...])
    up = _dot(xn, wu_ref[...])
    act = (_silu(gate) * up).astype(BF16)
    o_ref[...] += 0.5 * _dot(act, wd_ref[...])


def _ffn(x, g_all, wgu_all, wdn_all, layer, tm, tf):
    m, d = x.shape
    f = wdn_all.shape[1]
    nf = f // tf
    vmem = 4 * tm * d * 4 + tm * d * 2 + 2 * (2 * d * tf + tf * d) * 2 + 4 * tm * tf * 4
    return pl.pallas_call(
        _ffn_body,
        grid=(m // tm, nf),
        in_specs=[
            pl.BlockSpec((tm, d), lambda i, j: (i, 0)),
            pl.BlockSpec((None, 1, d), lambda i, j: (layer, 0, 0)),
            pl.BlockSpec((None, d, tf), lambda i, j: (layer, 0, j)),
            pl.BlockSpec((None, d, tf), lambda i, j: (layer, 0, nf + j)),
            pl.BlockSpec((None, tf, d), lambda i, j: (layer, j, 0)),
        ],
        out_specs=pl.BlockSpec((tm, d), lambda i, j: (i, 0)),
        out_shape=jax.ShapeDtypeStruct((m, d), F32),
        scratch_shapes=[pltpu.VMEM((tm, d), BF16)],
        compiler_params=_cparams(("parallel", "arbitrary"), vmem),
        name="half_ffn",
    )(x, g_all, wgu_all, wgu_all, wdn_all)


def _proj_body(x_ref, g_ref, w_ref, o_ref, xn_ref):
    @pl.when(pl.program_id(1) == 0)
    def _():
        xn_ref[...] = (_rms_rows(x_ref[...]) * g_ref[...]).astype(BF16)

    o_ref[...] = _dot(xn_ref[...], w_ref[...])


def _proj(x, g_all, w_all, layer, w_layer, tm, tn, col0=0, n_out=None):
    m, d = x.shape
    n_out = w_all.shape[2] - col0 if n_out is None else n_out
    jb = col0 // tn
    vmem = 4 * tm * d * 4 + tm * d * 2 + 2 * d * tn * 2 + 4 * tm * tn * 4
    return pl.pallas_call(
        _proj_body,
        grid=(m // tm, n_out // tn),
        in_specs=[
            pl.BlockSpec((tm, d), lambda i, j: (i, 0)),
            pl.BlockSpec((None, 1, d), lambda i, j: (layer, 0, 0)),
            pl.BlockSpec((None, d, tn), lambda i, j: (w_layer, 0, jb + j)),
        ],
        out_specs=pl.BlockSpec((tm, tn), lambda i, j: (i, j)),
        out_shape=jax.ShapeDtypeStruct((m, n_out), F32),
        scratch_shapes=[pltpu.VMEM((tm, d), BF16)],
        compiler_params=_cparams(("parallel", "arbitrary"), vmem),
        name="norm_proj",
    )(x, g_all, w_all)


def _qkproj_body(x_ref, g_ref, w_ref, nw_ref, cos_ref, sa_ref, sb_ref, o_ref, xn_ref):
    @pl.when(pl.program_id(1) == 0)
    def _():
        xn_ref[...] = (_rms_rows(x_ref[...]) * g_ref[...]).astype(BF16)

    xn = xn_ref[...]
    wn = nw_ref[...]
    cos, sa, sb = cos_ref[...], sa_ref[...], sb_ref[...]
    pair = 2 * DIFF_HEAD
    n_pairs = o_ref.shape[1] // pair

    def mm(g):
        return _dot(xn, w_ref[:, g * pair:(g + 1) * pair])

    acc_next = mm(0)
    for g in range(n_pairs):
        acc = acc_next
        if g + 1 < n_pairs:
            acc_next = mm(g + 1)
        for h in range(2):
            y = acc[:, h * DIFF_HEAD:(h + 1) * DIFF_HEAD]
            y = y * lax.rsqrt(_lane_sumsq(y) * (1.0 / DIFF_HEAD) + EPS) * wn
            y = (y * cos + pltpu.roll(y, DIFF_HEAD - ROPE_DIM // 2, 1) * sa
                 + pltpu.roll(y, ROPE_DIM // 2, 1) * sb)
            c0 = g * pair + h * DIFF_HEAD
            o_ref[:, c0:c0 + DIFF_HEAD] = y


def _qkproj(x, g_all, w_all, qk_norm, cos, sa, sb, layer, w_layer, tm, tn):
    m, d = x.shape
    nq = DIFF_QK_DIM // tn
    vmem = 4 * tm * d * 4 + tm * d * 2 + 2 * d * tn * 2 + 6 * tm * tn * 4 + 6 * tm * 128 * 4
    return pl.pallas_call(
        _qkproj_body,
        grid=(m // tm, 2 * nq),
        in_specs=[
            pl.BlockSpec((tm, d), lambda i, j: (i, 0)),
            pl.BlockSpec((None, 1, d), lambda i, j: (layer, 0, 0)),
            pl.BlockSpec((None, d, tn), lambda i, j: (w_layer, 0, j)),
            pl.BlockSpec((None, 1, DIFF_HEAD), lambda i, j: (2 * w_layer + j // nq, 0, 0)),
            pl.BlockSpec((tm, DIFF_HEAD), lambda i, j: (i, 0)),
            pl.BlockSpec((tm, DIFF_HEAD), lambda i, j: (i, 0)),
            pl.BlockSpec((tm, DIFF_HEAD), lambda i, j: (i, 0)),
        ],
        out_specs=pl.BlockSpec((None, tm, tn), lambda i, j: (j // nq, i, j % nq)),
        out_shape=jax.ShapeDtypeStruct((2, m, DIFF_QK_DIM), F32),
        scratch_shapes=[pltpu.VMEM((tm, d), BF16)],
        compiler_params=_cparams(("parallel", "arbitrary"), vmem),
        name="diff_qk_proj",
    )(x, g_all, w_all, qk_norm, cos, sa, sb)


def _outproj_body(y_ref, w_ref, x_ref, o_ref):
    o_ref[...] = x_ref[...] + _dot(y_ref[...], w_ref[...])


def _outproj(y, w_all, x, w_layer, tm, tn):
    m, k = y.shape
    d = x.shape[1]
    vmem = 2 * tm * k * 2 + 2 * k * tn * 2 + 5 * tm * tn * 4
    return pl.pallas_call(
        _outproj_body,
        grid=(m // tm, d // tn),
        in_specs=[
            pl.BlockSpec((tm, k), lambda i, j: (i, 0)),
            pl.BlockSpec((None, k, tn), lambda i, j: (w_layer, 0, j)),
            pl.BlockSpec((tm, tn), lambda i, j: (i, j)),
        ],
        out_specs=pl.BlockSpec((tm, tn), lambda i, j: (i, j)),
        out_shape=jax.ShapeDtypeStruct((m, d), F32),
        compiler_params=_cparams(("parallel", "arbitrary"), vmem),
        name="out_proj",
    )(y, w_all, x)


def _unit_lower_inverse_minus_identity(mms, level_masks, base_mask):
    ys = [-(mm * base_mask) for mm in mms]
    for mask in level_masks:
        os_ = [mm * mask for mm in mms]
        y_bf = [y.astype(BF16) for y in ys]
        zs = [o + _dot(yb, o.astype(BF16)) for o, yb in zip(os_, y_bf)]
        ys = [y - z - _dot(z.astype(BF16), yb) for y, z, yb in zip(ys, zs, y_bf)]
    return ys


def _gdn_body(qp_ref, kp_ref, vp_ref, z_ref, q8_ref, k8_ref, v8_ref, cwq_ref, cwk_ref, cwv_ref,
              b_ref, a_ref, alog_ref, dtb_ref, nw_ref, s0_ref,
              o_ref, sout_ref,
              q_s, k_s, v_s, u_s, w_s, at_s, grow_s, gcc_s, bc_s,
              *, t_len, t_valid, gw, ilv):
    C = gw
    HD = GDN_HEAD
    nc = t_len // C
    ng = t_len // gw
    cpg = gw // C
    nh = q_s.shape[1] // HD
    heads = tuple(range(2 * nh))
    rt = min(256, t_len)
    n_rt = t_len // rt

    def conv_tile(x_ref, p8_ref, cw_ref, r):
        head = p8_ref[...] if r == 0 else x_ref[r * rt - 8:r * rt, :]
        cur = x_ref[r * rt:(r + 1) * rt, :]
        ext = jnp.concatenate([head, cur], axis=0)
        w = cw_ref[...]
        y = ext[5:5 + rt, :] * w[0:1, :]
        y = y + ext[6:6 + rt, :] * w[1:2, :]
        y = y + ext[7:7 + rt, :] * w[2:3, :]
        y = y + cur * w[3:4, :]
        y = _silu(y)
        if t_valid < t_len:
            rows = r * rt + lax.broadcasted_iota(jnp.int32, y.shape, 0)
            y = jnp.where(rows < t_valid, y, 0.0)
        return y

    def l2n(x):
        return x * lax.rsqrt(jnp.sum(x * x, axis=-1, keepdims=True) + EPS)

    for r in range(n_rt):
        rows = slice(r * rt, (r + 1) * rt)
        q_t = conv_tile(qp_ref, q8_ref, cwq_ref, r)
        k_t = conv_tile(kp_ref, k8_ref, cwk_ref, r)
        for qh in range(nh):
            cs = slice(qh * HD, (qh + 1) * HD)
            q_s[rows, cs] = l2n(q_t[:, cs]) * (HD ** -0.5)
            k_s[rows, cs] = l2n(k_t[:, cs])
        v_s[rows, :] = conv_tile(vp_ref, v8_ref, cwv_ref, r)

    ri = lax.broadcasted_iota(jnp.int32, (gw, gw), 0)
    ci = lax.broadcasted_iota(jnp.int32, (gw, gw), 1)
    chunk_shift = C.bit_length() - 1
    same_chunk = (ri >> chunk_shift) == (ci >> chunk_shift)
    lower = jnp.logical_and(same_chunk, ci <= ri)
    strict = jnp.logical_and(same_chunk, ci < ri)
    upper_ones = jnp.where(jnp.logical_and(same_chunk, ri <= ci), 1.0, 0.0)
    eye = jnp.where(ri == ci, 1.0, 0.0)
    hi = lax.Precision.HIGHEST
    for h in heads:
        beta = jax.nn.sigmoid(b_ref[h])
        g = -jnp.exp(alog_ref[h]) * jax.nn.softplus(a_ref[h] + dtb_ref[h])
        if t_valid < t_len:
            pos = (lax.broadcasted_iota(jnp.int32, g.shape, 0) * gw
                   + lax.broadcasted_iota(jnp.int32, g.shape, 1))
            g = jnp.where(pos < t_valid, g, 0.0)
            beta = jnp.where(pos < t_valid, beta, 0.0)
        gc = jnp.dot(g, upper_ones, precision=hi, preferred_element_type=F32)
        grow_s[h] = gc
        gc_t = lax.dot_general(eye, gc, (((1,), (1,)), ((), ())), precision=hi,
                               preferred_element_type=F32)
        beta_t = lax.dot_general(eye, beta, (((1,), (1,)), ((), ())), precision=hi,
                                 preferred_element_type=F32)
        for i in range(ng):
            gcc_s[h, i * gw:(i + 1) * gw, :] = jnp.broadcast_to(gc_t[:, i:i + 1], (gw, HD))
            bc_s[h, i * gw:(i + 1) * gw, :] = jnp.broadcast_to(beta_t[:, i:i + 1], (gw, HD))

    base_mask = jnp.where((ri >> 1) == (ci >> 1), 1.0, 0.0)
    level_masks = []
    for s in range(1, chunk_shift):
        same_big = (ri >> (s + 1)) == (ci >> (s + 1))
        same_small = (ri >> s) == (ci >> s)
        level_masks.append(jnp.where(same_big, 1.0, 0.0) - jnp.where(same_small, 1.0, 0.0))

    def lane_tile(x):
        if gw <= HD:
            return x[:, :gw]
        return jnp.concatenate([x] * (gw // HD), axis=1)

    def solve_groups(it, carry):
        chains = []
        for gg in range(ilv):
            gi = it * ilv + gg
            r0 = pl.multiple_of(gi * gw, gw)
            for qh in range(nh):
                kc = k_s[pl.ds(r0, gw), qh * HD:(qh + 1) * HD]
                k_bf = kc.astype(BF16)
                qk = _dot_nt(q_s[pl.ds(r0, gw), qh * HD:(qh + 1) * HD].astype(BF16), k_bf)
                for h in (2 * qh, 2 * qh + 1):
                    gcc = gcc_s[h, pl.ds(r0, gw), :]
                    gd = lane_tile(gcc) - grow_s[h, pl.ds(gi, 1), :]
                    dec = jnp.where(lower, jnp.exp(jnp.where(lower, gd, 0.0)), 0.0)
                    beta_c = bc_s[h, pl.ds(r0, gw), :]
                    chains.append((h, r0, kc * beta_c, k_bf, dec, gcc, beta_c, qk))
        mms = [jnp.where(strict, _dot_nt(kb.astype(BF16), k_bf) * dec, 0.0)
               for (_, _, kb, k_bf, dec, _, _, _) in chains]
        ys = _unit_lower_inverse_minus_identity(mms, level_masks, base_mask)
        rhss = [jnp.concatenate([v_s[pl.ds(r0, gw), h * HD:(h + 1) * HD] * beta_c, kb * jnp.exp(gcc)], axis=1)
                for (h, r0, kb, _, _, gcc, beta_c, _) in chains]
        uws = [rhs + _dot(y.astype(BF16), rhs.astype(BF16)) for rhs, y in zip(rhss, ys)]
        for (h, r0, _, _, dec, _, _, qk), uw in zip(chains, uws):
            u_s[h, pl.ds(r0, gw), :] = uw[:, :HD]
            w_s[h, pl.ds(r0, gw), :] = uw[:, HD:].astype(BF16)
            at = qk * dec
            for k in range(cpg):
                at_s[h, pl.ds(pl.multiple_of(r0 + k * C, C), C), :] = (
                    at[k * C:(k + 1) * C, k * C:(k + 1) * C].astype(BF16))
        return carry

    lax.fori_loop(0, ng // ilv, solve_groups, 0)

    nw = nw_ref[...]

    def step(c, states):
        r0 = pl.multiple_of(c * C, C)
        kcs = [k_s[pl.ds(r0, C), qh * HD:(qh + 1) * HD] for qh in range(nh)]
        qcs = [q_s[pl.ds(r0, C), qh * HD:(qh + 1) * HD] for qh in range(nh)]
        gccs = [gcc_s[h, pl.ds(r0, C), :] for h in heads]
        lhss = [jnp.concatenate([w_s[h, pl.ds(r0, C), :], (qcs[h // 2] * jnp.exp(gccs[h])).astype(BF16)], axis=0)
                for h in heads]
        wss = [_dot(lhss[h], states[h].astype(BF16)) for h in heads]
        vns = [(u_s[h, pl.ds(r0, C), :] - wss[h][:C]).astype(BF16) for h in heads]
        glast = [gccs[h][C - 1:C, :] for h in heads]
        kes = [(kcs[h // 2] * jnp.exp(glast[h] - gccs[h])).astype(BF16) for h in heads]
        outs = [wss[h][C:] + _dot(at_s[h, pl.ds(r0, C), :], vns[h]) for h in heads]
        new_states = [states[h] * jnp.exp(glast[h]) + _dot_tn(kes[h], vns[h]) for h in heads]
        for h in heads:
            zc = z_ref[pl.ds(r0, C), h * HD:(h + 1) * HD]
            o_ref[pl.ds(r0, C), h * HD:(h + 1) * HD] = (_rms_rows(outs[h]) * nw * _silu(zc)).astype(BF16)
        return tuple(new_states)

    s_fin = lax.fori_loop(0, nc, step, tuple(s0_ref[h] for h in heads))
    for h in heads:
        sout_ref[h] = s_fin[h]


def _gdn_core(p, prev8, conv_w_all, b_rows, a_rows, alog_b, dtb_b, norm_w_all, s0, layer,
              batch, t_len, t_valid):
    HD = GDN_HEAD
    ng8, gw = b_rows.shape[2], b_rows.shape[3]
    C = gw
    ng = t_len // gw
    alog_b = jnp.broadcast_to(alog_b[:, :, None, None], alog_b.shape + (1, gw))
    dtb_b = jnp.broadcast_to(dtb_b[:, :, None, None], dtb_b.shape + (1, gw))
    step_bytes = 104 * t_len * HD
    nh = GDN_MAX_HEADS_PER_STEP
    while nh > 1 and step_bytes * nh > GDN_STEP_VMEM_BYTES:
        nh //= 2
    ilv = math.gcd(ng, max(1, GDN_SOLVE_INTERLEAVE // nh))
    qw, vw, nv = nh * HD, 2 * nh * HD, 2 * nh
    kq = GDN_KEY_DIM // qw
    kz = GDN_CONV_DIM // vw
    vmem = step_bytes * nh + 16 * 2**20
    grid = (batch, GDN_QK_HEADS // nh)
    return pl.pallas_call(
        functools.partial(_gdn_body, t_len=t_len, t_valid=t_valid, gw=gw, ilv=ilv),
        grid=grid,
        in_specs=[
            pl.BlockSpec((t_len, qw), lambda b, h: (b, h)),
            pl.BlockSpec((t_len, qw), lambda b, h: (b, kq + h)),
            pl.BlockSpec((t_len, vw), lambda b, h: (b, kq + h)),
            pl.BlockSpec((t_len, vw), lambda b, h: (b, kz + h)),
            pl.BlockSpec((None, 8, qw), lambda b, h: (b, 0, h)),
            pl.BlockSpec((None, 8, qw), lambda b, h: (b, 0, kq + h)),
            pl.BlockSpec((None, 8, vw), lambda b, h: (b, 0, kq + h)),
            pl.BlockSpec((None, GDN_CONV, qw), lambda b, h: (layer, 0, h)),
            pl.BlockSpec((None, GDN_CONV, qw), lambda b, h: (layer, 0, kq + h)),
            pl.BlockSpec((None, GDN_CONV, vw), lambda b, h: (layer, 0, kq + h)),
            pl.BlockSpec((None, nv, ng8, gw), lambda b, h: (b, h, 0, 0)),
            pl.BlockSpec((None, nv, ng8, gw), lambda b, h: (b, h, 0, 0)),
            pl.BlockSpec((None, nv, 1, gw), lambda b, h: (layer, h, 0, 0)),
            pl.BlockSpec((None, nv, 1, gw), lambda b, h: (layer, h, 0, 0)),
            pl.BlockSpec((None, 1, HD), lambda b, h: (layer, 0, 0)),
            pl.BlockSpec((None, nv, HD, HD), lambda b, h: (b, h, 0, 0)),
        ],
        out_specs=[
            pl.BlockSpec((t_len, vw), lambda b, h: (b, h)),
            pl.BlockSpec((None, nv, HD, HD), lambda b, h: (b, h, 0, 0)),
        ],
        out_shape=[
            jax.ShapeDtypeStruct((batch * t_len, GDN_VAL_DIM), BF16),
            jax.ShapeDtypeStruct((batch, GDN_V_HEADS, HD, HD), F32),
        ],
        scratch_shapes=[
            pltpu.VMEM((t_len, qw), F32),
            pltpu.VMEM((t_len, qw), F32),
            pltpu.VMEM((t_len, vw), F32),
            pltpu.VMEM((nv, t_len, HD), F32),
            pltpu.VMEM((nv, t_len, HD), BF16),
            pltpu.VMEM((nv, t_len, C), BF16),
            pltpu.VMEM((nv, ng8, gw), F32),
            pltpu.VMEM((nv, t_len, HD), F32),
            pltpu.VMEM((nv, t_len, HD), F32),
        ],
        compiler_params=_cparams(("parallel", "arbitrary"), vmem),
        name="gdn_core",
    )(p, p, p, p, prev8, prev8, prev8, conv_w_all, conv_w_all, conv_w_all,
      b_rows, a_rows, alog_b, dtb_b, norm_w_all, s0)


def _diff_lambda(lq1_ref, lk1_ref, lq2_ref, lk2_ref, lam_init):
    s1 = jnp.sum(lq1_ref[...] * lk1_ref[...], axis=-1, keepdims=True)
    s2 = jnp.sum(lq2_ref[...] * lk2_ref[...], axis=-1, keepdims=True)
    return jnp.exp(s1) - jnp.exp(s2) + lam_init


def _flash_body(q_ref, k_ref, v_ref, lq1_ref, lk1_ref, lq2_ref, lk2_ref, sn_ref, o_ref,
                m_s, l_s, acc_s, *, tq, sub, lam_init):
    qi = pl.program_id(2)
    scale = DIFF_HEAD ** -0.5
    q = q_ref[...]
    q_maps = (q[:, :DIFF_HEAD].astype(BF16), q[:, DIFF_HEAD:].astype(BF16))
    m_s[...] = jnp.full(m_s.shape, -jnp.inf, F32)
    l_s[...] = jnp.zeros(l_s.shape, F32)
    acc_s[...] = jnp.zeros(acc_s.shape, F32)
    c2 = scale * math.log2(math.e)
    maps = (0, 1)

    def block(kb, masked):
        r0 = pl.multiple_of(kb * tq, tq)
        k = k_ref[pl.ds(r0, tq), :]
        v = v_ref[pl.ds(r0, tq), :].astype(BF16)
        ks = [k[:, c * DIFF_HEAD:(c + 1) * DIFF_HEAD].astype(BF16) for c in maps]
        chains = [(r, c) for r in range(tq // sub) for c in maps]

        def scores(r, c):
            return _dot_nt(q_maps[c][r * sub:(r + 1) * sub, :], ks[c])

        s_next = scores(*chains[0])
        for i, (r, c) in enumerate(chains):
            s = s_next
            if i + 1 < len(chains):
                s_next = scores(*chains[i + 1])
            rs = slice(r * sub, (r + 1) * sub)
            if masked:
                keep = (lax.broadcasted_iota(jnp.int32, (sub, tq), 1)
                        <= lax.broadcasted_iota(jnp.int32, (sub, tq), 0) + r * sub)
                s = jnp.where(keep, s, -jnp.inf)
            m_prev = m_s[c, rs, :]
            m_new = jnp.maximum(m_prev, jnp.max(s, axis=-1, keepdims=True))
            p = jnp.exp2((s - jnp.concatenate([m_new] * (tq // V7X_LANES), axis=1)) * c2)
            alpha = jnp.exp2((m_prev - m_new) * c2)
            p_lanes = p[:, :V7X_LANES]
            for t in range(1, tq // V7X_LANES):
                p_lanes = p_lanes + p[:, t * V7X_LANES:(t + 1) * V7X_LANES]
            l_s[c, rs, :] = alpha * l_s[c, rs, :] + p_lanes
            acc_s[c, rs, :] = (jnp.concatenate([alpha] * (DIFF_V_HEAD // V7X_LANES), axis=1) * acc_s[c, rs, :]
                               + _dot(p.astype(BF16), v))
            m_s[c, rs, :] = m_new

    def body(kb, carry):
        block(kb, False)
        return carry

    lax.fori_loop(0, qi, body, 0)
    block(qi, True)
    lam = _diff_lambda(lq1_ref, lk1_ref, lq2_ref, lk2_ref, lam_init)
    l0 = jnp.sum(l_s[0], axis=-1, keepdims=True)
    l1 = jnp.sum(l_s[1], axis=-1, keepdims=True)
    o = acc_s[0] / l0 - lam * (acc_s[1] / l1)
    o_ref[...] = (_rms_rows(o) * sn_ref[...] * (1.0 - lam_init)).astype(BF16)


def _flash(qk, v, lq1, lk1, lq2, lk2, sn_all, w_layer, lam_init, batch, t_len, tq):
    nq = t_len // tq
    dv = DIFF_V_HEAD
    vmem = 2 * tq * dv * 4 + 4 * t_len * dv * 4 + 2 * tq * dv * 2 + 2 * tq * dv * 4 + 8 * tq * tq * 4 + 4 * 2**20
    lam_spec = pl.BlockSpec((None, 1, DIFF_HEAD), lambda b, h, i: (w_layer, 0, 0))
    return pl.pallas_call(
        functools.partial(_flash_body, tq=tq, sub=min(FLASH_SUB_ROWS, tq), lam_init=lam_init),
        grid=(batch, DIFF_HEADS, nq),
        in_specs=[
            pl.BlockSpec((None, tq, dv), lambda b, h, i: (0, b * nq + i, h)),
            pl.BlockSpec((None, t_len, dv), lambda b, h, i: (1, b, h)),
            pl.BlockSpec((t_len, dv), lambda b, h, i: (b, h)),
            lam_spec, lam_spec, lam_spec, lam_spec,
            pl.BlockSpec((None, 1, dv), lambda b, h, i: (w_layer, 0, 0)),
        ],
        out_specs=pl.BlockSpec((tq, dv), lambda b, h, i: (b * nq + i, h)),
        out_shape=jax.ShapeDtypeStruct((batch * t_len, DIFF_V_DIM), BF16),
        scratch_shapes=[
            pltpu.VMEM((2, tq, V7X_LANES), F32),
            pltpu.VMEM((2, tq, V7X_LANES), F32),
            pltpu.VMEM((2, tq, dv), F32),
        ],
        compiler_params=_cparams(("parallel", "parallel", "arbitrary"), vmem),
        name="diff_flash",
    )(qk, qk, v, lq1, lk1, lq2, lk2, sn_all)


def _decode_body(pt_ref, q_ref, kn_ref, vn_ref, *rest, n_steps, pps, n_new, lam_init):
    del pt_ref
    kc_refs, vc_refs = rest[:pps], rest[pps:2 * pps]
    lq1_ref, lk1_ref, lq2_ref, lk2_ref, sn_ref, o_ref, m_s, l_s, acc_s = rest[2 * pps:]
    p = pl.program_id(1)
    scale = DIFF_HEAD ** -0.5
    n_half = q_ref.shape[1]
    n_rows = 2 * n_half
    n_cols = vn_ref.shape[0]

    @pl.when(p == 0)
    def _():
        m_s[...] = jnp.full(m_s.shape, -jnp.inf, F32)
        l_s[...] = jnp.zeros(l_s.shape, F32)
        acc_s[...] = jnp.zeros(acc_s.shape, F32)

    row_head = (lax.broadcasted_iota(jnp.int32, (n_rows, n_cols), 0) % n_half) // n_new
    col = lax.broadcasted_iota(jnp.int32, (n_rows, n_cols), 1)
    head_ok = row_head == col % DIFF_HEADS

    q_bf = [q_ref[c].astype(BF16) for c in range(2)]

    def update(k_refs, v_refs, mask):
        ss = [jnp.concatenate(
            [_dot_nt(q_bf[c], k_ref[pl.ds(c, n_cols, stride=2), :].astype(BF16)) for c in range(2)],
            axis=0) * scale for k_ref in k_refs]
        ss = [jnp.where(mask, s, -jnp.inf) for s in ss]
        m_prev = m_s[...]
        m_new = m_prev
        for s in ss:
            m_new = jnp.maximum(m_new, jnp.max(s, axis=-1, keepdims=True))
        alpha = jnp.exp(m_prev - m_new)
        prs = [jnp.exp(s - m_new) for s in ss]
        l_new = alpha * l_s[...]
        for pr in prs:
            l_new = l_new + jnp.sum(pr, axis=-1, keepdims=True)
        pvs = [_dot(pr.astype(BF16), v_ref[...].astype(BF16)) for pr, v_ref in zip(prs, v_refs)]
        acc = alpha * acc_s[...]
        for pv in pvs:
            acc = acc + pv
        l_s[...] = l_new
        m_s[...] = m_new
        acc_s[...] = acc

    update(kc_refs, vc_refs, head_ok)

    @pl.when(p == n_steps - 1)
    def _():
        qidx = lax.broadcasted_iota(jnp.int32, (n_rows, n_cols), 0) % n_new
        causal = col // DIFF_HEADS <= qidx
        update([kn_ref], [vn_ref], jnp.logical_and(head_ok, causal))
        lam = _diff_lambda(lq1_ref, lk1_ref, lq2_ref, lk2_ref, lam_init)
        o = acc_s[...] / l_s[...]
        od = o[:n_half] - lam * o[n_half:]
        o_ref[...] = _rms_rows(od) * sn_ref[...] * (1.0 - lam_init)


def _decode(page_table, q_maps, kn_pad, vn_pad, cache_k, cache_v, lq1, lk1, lq2, lk2, sn_all,
            w_layer, lam_init, n_new):
    bs, n_pages = page_table.shape
    n_half = q_maps.shape[2]
    k_rows, v_rows = cache_k.shape[2], cache_v.shape[2]
    pps = math.gcd(n_pages, DECODE_PAGES_PER_STEP)
    n_steps = n_pages // pps
    lam_spec = pl.BlockSpec((None, 1, DIFF_HEAD), lambda b, p, pt: (w_layer, 0, 0))
    vmem = 3 * pps * (k_rows * DIFF_HEAD + v_rows * DIFF_V_HEAD) * 4 + 20 * 2**20

    def page_spec(rows, width, i):
        return pl.BlockSpec((None, None, rows, width), lambda b, p, pt: (w_layer, pt[b, p * pps + i], 0, 0))

    grid_spec = pltpu.PrefetchScalarGridSpec(
        num_scalar_prefetch=1,
        grid=(bs, n_steps),
        in_specs=[
            pl.BlockSpec((None, 2, n_half, DIFF_HEAD), lambda b, p, pt: (b, 0, 0, 0)),
            pl.BlockSpec((None, k_rows, DIFF_HEAD), lambda b, p, pt: (b, 0, 0)),
            pl.BlockSpec((None, v_rows, DIFF_V_HEAD), lambda b, p, pt: (b, 0, 0)),
            *[page_spec(k_rows, DIFF_HEAD, i) for i in range(pps)],
            *[page_spec(v_rows, DIFF_V_HEAD, i) for i in range(pps)],
            lam_spec, lam_spec, lam_spec, lam_spec,
            pl.BlockSpec((None, 1, DIFF_V_HEAD), lambda b, p, pt: (w_layer, 0, 0)),
        ],
        out_specs=pl.BlockSpec((None, n_half, DIFF_V_HEAD), lambda b, p, pt: (b, 0, 0)),
        scratch_shapes=[
            pltpu.VMEM((2 * n_half, 1), F32),
            pltpu.VMEM((2 * n_half, 1), F32),
            pltpu.VMEM((2 * n_half, DIFF_V_HEAD), F32),
        ],
    )
    return pl.pallas_call(
        functools.partial(_decode_body, n_steps=n_steps, pps=pps, n_new=n_new, lam_init=lam_init),
        grid_spec=grid_spec,
        out_shape=jax.ShapeDtypeStruct((bs, n_half, DIFF_V_HEAD), F32),
        compiler_params=_cparams(("parallel", "arbitrary"), vmem),
        name="diff_decode",
    )(page_table, q_maps, kn_pad, vn_pad, *([cache_k] * pps), *([cache_v] * pps), lq1, lk1, lq2, lk2, sn_all)


def _rope_tables(pos):
    half = ROPE_DIM // 2
    inv = ROPE_THETA ** (-jnp.arange(half, dtype=F32) * 2.0 / ROPE_DIM)
    ang = pos.astype(F32)[:, None] * inv[None, :]
    cos, sin = jnp.cos(ang), jnp.sin(ang)
    n = pos.shape[0]
    tail = DIFF_HEAD - ROPE_DIM
    c = jnp.concatenate([cos, cos, jnp.ones((n, tail), F32)], axis=1)
    sa = jnp.concatenate([-sin, jnp.zeros((n, half + tail), F32)], axis=1)
    sb = jnp.concatenate([jnp.zeros((n, half), F32), sin, jnp.zeros((n, tail), F32)], axis=1)
    return c, sa, sb


def _gate_rows(p, batch, t_len):
    gw = min(GDN_SOLVE_WIDTH, t_len)
    ng = t_len // gw
    ba = p[:, GDN_CONV_DIM + GDN_VAL_DIM:GDN_CONV_DIM + GDN_VAL_DIM + 2 * GDN_V_HEADS]
    ba = ba.reshape(batch, ng, gw, 2, GDN_V_HEADS).transpose(3, 0, 4, 1, 2)
    ng8 = -(-ng // 8) * 8
    ba = jnp.pad(ba, ((0, 0), (0, 0), (0, 0), (0, ng8 - ng), (0, 0)))
    return ba[0], ba[1]


def kernel(x_prompt, x_sample, state_delta, state_conv, cache_k, cache_v, page_table, norm_ffn1, ffn1_w_gu, ffn1_w_dn, norm_mix, norm_ffn2, ffn2_w_gu, ffn2_w_dn, gdn_w_in, gdn_conv_w, gdn_a_log, gdn_dt_bias, gdn_norm_w, gdn_w_out, diff_w_in, diff_q_norm, diff_k_norm, diff_lam_q1, diff_lam_k1, diff_lam_q2, diff_lam_k2, diff_sub_norm, diff_w_out):
    bp, tp, d = x_prompt.shape
    bs, ts, _ = x_sample.shape
    depth = norm_ffn1.shape[0]
    n_pages = page_table.shape[1]
    past = n_pages * PAGE_SIZE
    C = GDN_CHUNK

    bf = lambda w: w.astype(BF16)
    row = lambda w: w.reshape(w.shape[0], 1, w.shape[1])
    w_gu1, w_dn1, w_gu2, w_dn2 = bf(ffn1_w_gu), bf(ffn1_w_dn), bf(ffn2_w_gu), bf(ffn2_w_dn)
    tn = 512
    tn_gdn = 1280
    gdn_n = -(-GDN_IN_DIM // tn_gdn) * tn_gdn
    w_gin = jnp.pad(bf(gdn_w_in), ((0, 0), (0, 0), (0, gdn_n - GDN_IN_DIM)))
    w_gout, w_din, w_dout = bf(gdn_w_out), bf(diff_w_in), bf(diff_w_out)
    g_ffn1, g_mix, g_ffn2 = row(norm_ffn1), row(norm_mix), row(norm_ffn2)
    alog_b, dtb_b = gdn_a_log, gdn_dt_bias
    gdn_nw = row(gdn_norm_w)
    qk_n = jnp.stack([diff_q_norm, diff_k_norm], axis=1).reshape(-1, 1, DIFF_HEAD)
    lq1, lk1, lq2, lk2 = row(diff_lam_q1), row(diff_lam_k1), row(diff_lam_q2), row(diff_lam_k2)
    sub_n = row(diff_sub_norm)
    ck = cache_k.reshape(cache_k.shape[0], cache_k.shape[1], PAGE_SIZE * DIFF_HEADS * 2, DIFF_HEAD)
    cv = cache_v.reshape(cache_v.shape[0], cache_v.shape[1], PAGE_SIZE * DIFF_HEADS, DIFF_V_HEAD)

    rope_p = _rope_tables(jnp.arange(tp))
    rope_p = tuple(jnp.tile(t, (bp, 1)) for t in rope_p)
    rope_s = _rope_tables(past + jnp.arange(ts))
    rope_s = tuple(jnp.tile(t, (bs, 1)) for t in rope_s)

    xp = x_prompt.reshape(bp * tp, d)
    xs = x_sample.reshape(bs * ts, d)
    mp, ms = bp * tp, bs * ts
    tm_p, tm_big, tm_s = 1024, 1024, ms
    tf = 512
    ts_pad = -(-ts // C) * C

    prev8_p = jnp.zeros((bp, 8, GDN_CONV_DIM), F32)
    s0_p = jnp.zeros((bp, GDN_V_HEADS, GDN_HEAD, GDN_HEAD), F32)

    p_delta, p_conv, p_k, p_v = [], [], [], []
    s_delta, s_conv, s_k, s_v = [], [], [], []
    for i in range(depth):
        xp = _ffn(xp, g_ffn1, w_gu1, w_dn1, i, tm_p, tf)
        xs = _ffn(xs, g_ffn1, w_gu1, w_dn1, i, tm_s, tf)
        j = i // 2
        if i % 2 == 0:
            pp = _proj(xp, g_mix, w_gin, i, j, tm_big, tn_gdn)
            ps = _proj(xs, g_mix, w_gin, i, j, tm_s, tn_gdn)
            b_rows, a_rows = _gate_rows(pp, bp, tp)
            op, sp = _gdn_core(pp, prev8_p, gdn_conv_w, b_rows, a_rows, alog_b, dtb_b, gdn_nw, s0_p, j,
                               bp, tp, tp)
            p_conv.append(pp.reshape(bp, tp, gdn_n)[:, tp - (GDN_CONV - 1):, :GDN_CONV_DIM])
            p_delta.append(sp)
            xp = _outproj(op, w_gout, xp, j, tm_big, tn)

            ps3 = ps.reshape(bs, ts, gdn_n)
            ps_pad = jnp.pad(ps3, ((0, 0), (0, ts_pad - ts), (0, 0))).reshape(bs * ts_pad, gdn_n)
            prev8_s = jnp.pad(state_conv[j], ((0, 0), (8 - (GDN_CONV - 1), 0), (0, 0)))
            b_rows, a_rows = _gate_rows(ps_pad, bs, ts_pad)
            os_, ss = _gdn_core(ps_pad, prev8_s, gdn_conv_w, b_rows, a_rows, alog_b, dtb_b, gdn_nw,
                                state_delta[j], j, bs, ts_pad, ts)
            xp_conv = jnp.concatenate([state_conv[j], ps3[:, :, :GDN_CONV_DIM]], axis=1)
            s_conv.append(xp_conv[:, ts:])
            s_delta.append(ss)
            os_ = os_.reshape(bs, ts_pad, GDN_VAL_DIM)[:, :ts].reshape(ms, GDN_VAL_DIM)
            xs = _outproj(os_, w_gout, xs, j, tm_s, tn)
        else:
            lam_init = 0.8 - 0.6 * math.exp(-0.3 * i)
            qkp = _qkproj(xp, g_mix, w_din, qk_n, *rope_p, i, j, tm_big, tn)
            vp = _proj(xp, g_mix, w_din, i, j, tm_big, tn, col0=2 * DIFF_QK_DIM, n_out=DIFF_V_DIM)
            qks = _qkproj(xs, g_mix, w_din, qk_n, *rope_s, i, j, tm_s, tn)
            vs = _proj(xs, g_mix, w_din, i, j, tm_s, tn, col0=2 * DIFF_QK_DIM, n_out=DIFF_V_DIM)
            qs, ks = qks[0], qks[1]
            op = _flash(qkp, vp, lq1, lk1, lq2, lk2, sub_n, j, lam_init, bp, tp, 512)
            xp = _outproj(op, w_dout, xp, j, tm_big, tn)
            p_k.append(qkp[1].reshape(bp, tp, DIFF_HEADS, 2, DIFF_HEAD))
            p_v.append(vp.reshape(bp, tp, DIFF_HEADS, DIFF_V_HEAD))

            q_maps = qs.reshape(bs, ts, DIFF_HEADS, 2, DIFF_HEAD).transpose(0, 3, 2, 1, 4)
            q_maps = q_maps.reshape(bs, 2, DIFF_HEADS * ts, DIFF_HEAD)
            kn_pad = jnp.pad(ks.reshape(bs, ts * DIFF_HEADS * 2, DIFF_HEAD),
                             ((0, 0), (0, (PAGE_SIZE - ts) * DIFF_HEADS * 2), (0, 0)))
            vn_pad = jnp.pad(vs.reshape(bs, ts * DIFF_HEADS, DIFF_V_HEAD),
                             ((0, 0), (0, (PAGE_SIZE - ts) * DIFF_HEADS), (0, 0)))
            od = _decode(page_table, q_maps, kn_pad, vn_pad, ck, cv, lq1, lk1, lq2, lk2, sub_n,
                         j, lam_init, ts)
            od = od.reshape(bs, DIFF_HEADS, ts, DIFF_V_HEAD).transpose(0, 2, 1, 3).reshape(ms, DIFF_V_DIM)
            xs = _outproj(od.astype(BF16), w_dout, xs, j, tm_s, tn)
            s_k.append(ks.reshape(bs, ts, DIFF_HEADS, 2, DIFF_HEAD))
            s_v.append(vs.reshape(bs, ts, DIFF_HEADS, DIFF_V_HEAD))
        xp = _ffn(xp, g_ffn2, w_gu2, w_dn2, i, tm_p, tf)
        xs = _ffn(xs, g_ffn2, w_gu2, w_dn2, i, tm_s, tf)

    return (xp.reshape(bp, tp, d), xs.reshape(bs, ts, d),
            jnp.stack(p_delta), jnp.stack(p_conv), jnp.stack(p_k), jnp.stack(p_v),
            jnp.stack(s_delta), jnp.stack(s_conv), jnp.stack(s_k), jnp.stack(s_v))
```

```python
import functools
import math

import jax
import jax.numpy as jnp
from jax import lax
from jax.experimental import pallas as pl
from jax.experimental.pallas import tpu as pltpu

F32 = jnp.float32
BF16 = jnp.bfloat16
EPS = 1e-6

D_MODEL = 2048
D_FF = 5632
GDN_QK_HEADS = 16
GDN_V_HEADS = 32
GDN_HEAD = 128
GDN_CONV = 4
GDN_CHUNK = 64
GDN_KEY_DIM = GDN_QK_HEADS * GDN_HEAD
GDN_VAL_DIM = GDN_V_HEADS * GDN_HEAD
GDN_CONV_DIM = 2 * GDN_KEY_DIM + GDN_VAL_DIM
GDN_IN_DIM = GDN_CONV_DIM + GDN_VAL_DIM + 2 * GDN_V_HEADS
DIFF_HEADS = 8
DIFF_HEAD = 128
DIFF_V_HEAD = 2 * DIFF_HEAD
DIFF_QK_DIM = DIFF_HEADS * 2 * DIFF_HEAD
DIFF_V_DIM = DIFF_HEADS * DIFF_V_HEAD
ROPE_DIM = DIFF_HEAD // 4
ROPE_THETA = 500000.0
PAGE_SIZE = 128

V7X_LANES = 128
V7X_VMEM_LIMIT_BYTES = 56 * 1024 * 1024

GDN_SOLVE_WIDTH = 128
GDN_SOLVE_INTERLEAVE = 4
GDN_MAX_HEADS_PER_STEP = 4
GDN_CONV_ROWS = 64
GDN_CONV_TILES_PER_TRIP = 32
GDN_STEP_VMEM_BYTES = 40 * 1024 * 1024
DECODE_PAGES_PER_STEP = 4
FLASH_SUB_ROWS = 256


def _cparams(semantics, vmem_bytes):
    return pltpu.CompilerParams(dimension_semantics=semantics,
                                vmem_limit_bytes=int(min(max(vmem_bytes, 16 * 2**20), V7X_VMEM_LIMIT_BYTES)))


def _dot(a, b):
    return jnp.dot(a, b, preferred_element_type=F32)


def _dot_nt(a, b):
    return lax.dot_general(a, b, (((1,), (1,)), ((), ())), preferred_element_type=F32)


def _dot_tn(a, b):
    return lax.dot_general(a, b, (((0,), (0,)), ((), ())), preferred_element_type=F32)


def _rms_rows(x):
    return x * lax.rsqrt(jnp.mean(x * x, axis=-1, keepdims=True) + EPS)


def _silu(x):
    return x * jax.nn.sigmoid(x)


def _lane_sumsq(x):
    ones = jnp.ones((x.shape[1], V7X_LANES), BF16)
    return _dot((x * x).astype(BF16), ones)


def _ffn_body(x_ref, g_ref, wg_ref, wu_ref, wd_ref, o_ref, xn_ref):
    j = pl.program_id(1)

    @pl.when(j == 0)
    def _():
        x = x_ref[...]
        xn_ref[...] = (_rms_rows(x) * g_ref[...]).astype(BF16)
        o_ref[...] = x

    xn = xn_ref[...]
    gate = _dot(xn, wg_ref[...].astype(BF16))
    up = _dot(xn, wu_ref[...].astype(BF16))
    act = (_silu(gate) * up).astype(BF16)
    o_ref[...] += 0.5 * _dot(act, wd_ref[...].astype(BF16))


def _ffn(x, g_all, wgu_all, wdn_all, layer, tm, tf):
    m, d = x.shape
    f = wdn_all.shape[1]
    nf = f // tf
    w_bytes = wgu_all.dtype.itemsize
    vmem = 4 * tm * d * 4 + tm * d * 2 + 2 * (2 * d * tf + tf * d) * w_bytes + 6 * tm * tf * 4
    return pl.pallas_call(
        _ffn_body,
        grid=(m // tm, nf),
        in_specs=[
            pl.BlockSpec((tm, d), lambda i, j: (i, 0)),
            pl.BlockSpec((None, 1, d), lambda i, j: (layer, 0, 0)),
            pl.BlockSpec((None, d, tf), lambda i, j: (layer, 0, j)),
            pl.BlockSpec((None, d, tf), lambda i, j: (layer, 0, nf + j)),
            pl.BlockSpec((None, tf, d), lambda i, j: (layer, j, 0)),
        ],
        out_specs=pl.BlockSpec((tm, d), lambda i, j: (i, 0)),
        out_shape=jax.ShapeDtypeStruct((m, d), F32),
        scratch_shapes=[pltpu.VMEM((tm, d), BF16)],
        compiler_params=_cparams(("parallel", "arbitrary"), vmem),
        name="half_ffn",
    )(x, g_all, wgu_all, wgu_all, wdn_all)


def _proj_body(x_ref, g_ref, w_ref, o_ref, xn_ref):
    @pl.when(pl.program_id(1) == 0)
    def _():
        xn_ref[...] = (_rms_rows(x_ref[...]) * g_ref[...]).astype(BF16)

    o_ref[...] = _dot(xn_ref[...], w_ref[...].astype(BF16))


def _proj(x, g_all, w_all, layer, w_layer, tm, tn, col0=0, n_out=None):
    m, d = x.shape
    n_out = w_all.shape[2] - col0 if n_out is None else n_out
    jb = col0 // tn
    vmem = 4 * tm * d * 4 + tm * d * 2 + 2 * d * tn * w_all.dtype.itemsize + 4 * tm * tn * 4 + d * tn * 2
    return pl.pallas_call(
        _proj_body,
        grid=(m // tm, n_out // tn),
        in_specs=[
            pl.BlockSpec((tm, d), lambda i, j: (i, 0)),
            pl.BlockSpec((None, 1, d), lambda i, j: (layer, 0, 0)),
            pl.BlockSpec((None, d, tn), lambda i, j: (w_layer, 0, jb + j)),
        ],
        out_specs=pl.BlockSpec((tm, tn), lambda i, j: (i, j)),
        out_shape=jax.ShapeDtypeStruct((m, n_out), F32),
        scratch_shapes=[pltpu.VMEM((tm, d), BF16)],
        compiler_params=_cparams(("parallel", "arbitrary"), vmem),
        name="norm_proj",
    )(x, g_all, w_all)


def _qkproj_body(x_ref, g_ref, w_ref, nw_ref, cos_ref, sa_ref, sb_ref, o_ref, xn_ref):
    @pl.when(pl.program_id(1) == 0)
    def _():
        xn_ref[...] = (_rms_rows(x_ref[...]) * g_ref[...]).astype(BF16)

    xn = xn_ref[...]
    wn = nw_ref[...]
    cos, sa, sb = cos_ref[...], sa_ref[...], sb_ref[...]
    pair = 2 * DIFF_HEAD
    n_pairs = o_ref.shape[1] // pair

    def mm(g):
        return _dot(xn, w_ref[:, g * pair:(g + 1) * pair].astype(BF16))

    acc_next = mm(0)
    for g in range(n_pairs):
        acc = acc_next
        if g + 1 < n_pairs:
            acc_next = mm(g + 1)
        for h in range(2):
            y = acc[:, h * DIFF_HEAD:(h + 1) * DIFF_HEAD]
            y = y * lax.rsqrt(_lane_sumsq(y) * (1.0 / DIFF_HEAD) + EPS) * wn
            y = (y * cos + pltpu.roll(y, DIFF_HEAD - ROPE_DIM // 2, 1) * sa
                 + pltpu.roll(y, ROPE_DIM // 2, 1) * sb)
            c0 = g * pair + h * DIFF_HEAD
            o_ref[:, c0:c0 + DIFF_HEAD] = y


def _qkproj(x, g_all, w_all, qk_norm, cos, sa, sb, layer, w_layer, tm, tn):
    m, d = x.shape
    nq = DIFF_QK_DIM // tn
    vmem = (4 * tm * d * 4 + tm * d * 2 + 2 * d * tn * w_all.dtype.itemsize + 6 * tm * tn * 4
            + 6 * tm * 128 * 4 + d * tn * 2)
    return pl.pallas_call(
        _qkproj_body,
        grid=(m // tm, 2 * nq),
        in_specs=[
            pl.BlockSpec((tm, d), lambda i, j: (i, 0)),
            pl.BlockSpec((None, 1, d), lambda i, j: (layer, 0, 0)),
            pl.BlockSpec((None, d, tn), lambda i, j: (w_layer, 0, j)),
            pl.BlockSpec((None, 1, DIFF_HEAD), lambda i, j: (2 * w_layer + j // nq, 0, 0)),
            pl.BlockSpec((tm, DIFF_HEAD), lambda i, j: (i, 0)),
            pl.BlockSpec((tm, DIFF_HEAD), lambda i, j: (i, 0)),
            pl.BlockSpec((tm, DIFF_HEAD), lambda i, j: (i, 0)),
        ],
        out_specs=pl.BlockSpec((None, tm, tn), lambda i, j: (j // nq, i, j % nq)),
        out_shape=jax.ShapeDtypeStruct((2, m, DIFF_QK_DIM), F32),
        scratch_shapes=[pltpu.VMEM((tm, d), BF16)],
        compiler_params=_cparams(("parallel", "arbitrary"), vmem),
        name="diff_qk_proj",
    )(x, g_all, w_all, qk_norm, cos, sa, sb)


def _outproj_body(y_ref, w_ref, x_ref, o_ref):
    o_ref[...] = x_ref[...] + _dot(y_ref[...], w_ref[...].astype(BF16))


def _outproj(y, w_all, x, w_layer, tm, tn):
    m, k = y.shape
    d = x.shape[1]
    vmem = 2 * tm * k * 2 + 2 * k * tn * w_all.dtype.itemsize + 6 * tm * tn * 4 + k * tn * 2
    return pl.pallas_call(
        _outproj_body,
        grid=(m // tm, d // tn),
        in_specs=[
            pl.BlockSpec((tm, k), lambda i, j: (i, 0)),
            pl.BlockSpec((None, k, tn), lambda i, j: (w_layer, 0, j)),
            pl.BlockSpec((tm, tn), lambda i, j: (i, j)),
        ],
        out_specs=pl.BlockSpec((tm, tn), lambda i, j: (i, j)),
        out_shape=jax.ShapeDtypeStruct((m, d), F32),
        compiler_params=_cparams(("parallel", "arbitrary"), vmem),
        name="out_proj",
    )(y, w_all, x)


def _unit_lower_inverse_minus_identity(mms, level_masks, base_mask):
    ys = [-(mm * base_mask) for mm in mms]
    for mask in level_masks:
        os_ = [mm * mask for mm in mms]
        y_bf = [y.astype(BF16) for y in ys]
        zs = [o + _dot(yb, o.astype(BF16)) for o, yb in zip(os_, y_bf)]
        ys = [y - z - _dot(z.astype(BF16), yb) for y, z, yb in zip(ys, zs, y_bf)]
    return ys


def _gdn_body(qp_ref, kp_ref, vp_ref, z_ref, q8_ref, k8_ref, v8_ref, cwq_ref, cwk_ref, cwv_ref,
              b_ref, a_ref, alog_ref, dtb_ref, nw_ref, s0_ref,
              o_ref, sout_ref,
              q_s, k_s, v_s, u_s, w_s, at_s, grow_s, gcc_s, bc_s,
              *, t_len, t_valid, gw, ilv):
    C = gw
    HD = GDN_HEAD
    nc = t_len // C
    ng = t_len // gw
    cpg = gw // C
    nh = q_s.shape[1] // HD
    heads = tuple(range(2 * nh))
    rt = min(GDN_CONV_ROWS, t_len)

    def conv_block(x_ref, p8_ref, cw_ref, r0, cs, first):
        head = p8_ref[:, cs] if first else x_ref[pl.ds(r0 - 8, 8), cs]
        cur = x_ref[pl.ds(r0, rt), cs]
        ext = jnp.concatenate([head, cur], axis=0)
        w = cw_ref[:, cs]
        y = ext[5:5 + rt, :] * w[0:1, :]
        y = y + ext[6:6 + rt, :] * w[1:2, :]
        y = y + ext[7:7 + rt, :] * w[2:3, :]
        y = y + cur * w[3:4, :]
        y = _silu(y)
        if t_valid < t_len:
            rows = r0 + lax.broadcasted_iota(jnp.int32, y.shape, 0)
            y = jnp.where(rows < t_valid, y, 0.0)
        return y

    def l2n(x):
        return x * lax.rsqrt(jnp.sum(x * x, axis=-1, keepdims=True) + EPS)

    def conv_rows(r0, first):
        for qh in range(nh):
            cs = slice(qh * HD, (qh + 1) * HD)
            q_s[pl.ds(r0, rt), cs] = l2n(conv_block(qp_ref, q8_ref, cwq_ref, r0, cs, first)) * (HD ** -0.5)
            k_s[pl.ds(r0, rt), cs] = l2n(conv_block(kp_ref, k8_ref, cwk_ref, r0, cs, first))
        for h in heads:
            cs = slice(h * HD, (h + 1) * HD)
            v_s[pl.ds(r0, rt), cs] = conv_block(vp_ref, v8_ref, cwv_ref, r0, cs, first)

    per_trip = math.gcd(t_len // rt, GDN_CONV_TILES_PER_TRIP)
    conv_rows(0, True)
    for s in range(1, per_trip):
        conv_rows(s * rt, False)

    def conv_body(r, carry):
        for s in range(per_trip):
            conv_rows(pl.multiple_of((r * per_trip + s) * rt, rt), False)
        return carry

    lax.fori_loop(1, t_len // (rt * per_trip), conv_body, 0)

    ri = lax.broadcasted_iota(jnp.int32, (gw, gw), 0)
    ci = lax.broadcasted_iota(jnp.int32, (gw, gw), 1)
    chunk_shift = C.bit_length() - 1
    same_chunk = (ri >> chunk_shift) == (ci >> chunk_shift)
    lower = jnp.logical_and(same_chunk, ci <= ri)
    strict = jnp.logical_and(same_chunk, ci < ri)
    upper_ones = jnp.where(jnp.logical_and(same_chunk, ri <= ci), 1.0, 0.0)
    eye = jnp.where(ri == ci, 1.0, 0.0)
    hi = lax.Precision.HIGHEST
    for h in heads:
        beta = jax.nn.sigmoid(b_ref[h])
        g = -jnp.exp(alog_ref[h]) * jax.nn.softplus(a_ref[h] + dtb_ref[h])
        if t_valid < t_len:
            pos = (lax.broadcasted_iota(jnp.int32, g.shape, 0) * gw
                   + lax.broadcasted_iota(jnp.int32, g.shape, 1))
            g = jnp.where(pos < t_valid, g, 0.0)
            beta = jnp.where(pos < t_valid, beta, 0.0)
        gc = jnp.dot(g, upper_ones, precision=hi, preferred_element_type=F32)
        grow_s[h] = gc
        gc_t = lax.dot_general(eye, gc, (((1,), (1,)), ((), ())), precision=hi,
                               preferred_element_type=F32)
        beta_t = lax.dot_general(eye, beta, (((1,), (1,)), ((), ())), precision=hi,
                                 preferred_element_type=F32)
        for i in range(ng):
            gcc_s[h, i * gw:(i + 1) * gw, :] = jnp.broadcast_to(gc_t[:, i:i + 1], (gw, HD))
            bc_s[h, i * gw:(i + 1) * gw, :] = jnp.broadcast_to(beta_t[:, i:i + 1], (gw, HD))

    base_mask = jnp.where((ri >> 1) == (ci >> 1), 1.0, 0.0)
    level_masks = []
    for s in range(1, chunk_shift):
        same_big = (ri >> (s + 1)) == (ci >> (s + 1))
        same_small = (ri >> s) == (ci >> s)
        level_masks.append(jnp.where(same_big, 1.0, 0.0) - jnp.where(same_small, 1.0, 0.0))

    def lane_tile(x):
        if gw <= HD:
            return x[:, :gw]
        return jnp.concatenate([x] * (gw // HD), axis=1)

    def solve_groups(it, carry):
        chains = []
        for gg in range(ilv):
            gi = it * ilv + gg
            r0 = pl.multiple_of(gi * gw, gw)
            for qh in range(nh):
                kc = k_s[pl.ds(r0, gw), qh * HD:(qh + 1) * HD]
                k_bf = kc.astype(BF16)
                qk = _dot_nt(q_s[pl.ds(r0, gw), qh * HD:(qh + 1) * HD].astype(BF16), k_bf)
                for h in (2 * qh, 2 * qh + 1):
                    gcc = gcc_s[h, pl.ds(r0, gw), :]
                    gd = lane_tile(gcc) - grow_s[h, pl.ds(gi, 1), :]
                    dec = jnp.where(lower, jnp.exp(jnp.where(lower, gd, 0.0)), 0.0)
                    beta_c = bc_s[h, pl.ds(r0, gw), :]
                    chains.append((h, r0, kc * beta_c, k_bf, dec, gcc, beta_c, qk))
        mms = [jnp.where(strict, _dot_nt(kb.astype(BF16), k_bf) * dec, 0.0)
               for (_, _, kb, k_bf, dec, _, _, _) in chains]
        ys = _unit_lower_inverse_minus_identity(mms, level_masks, base_mask)
        rhss = [jnp.concatenate([v_s[pl.ds(r0, gw), h * HD:(h + 1) * HD] * beta_c, kb * jnp.exp(gcc)], axis=1)
                for (h, r0, kb, _, _, gcc, beta_c, _) in chains]
        uws = [rhs + _dot(y.astype(BF16), rhs.astype(BF16)) for rhs, y in zip(rhss, ys)]
        for (h, r0, _, _, dec, _, _, qk), uw in zip(chains, uws):
            u_s[h, pl.ds(r0, gw), :] = uw[:, :HD]
            w_s[h, pl.ds(r0, gw), :] = uw[:, HD:].astype(BF16)
            at = qk * dec
            for k in range(cpg):
                at_s[h, pl.ds(pl.multiple_of(r0 + k * C, C), C), :] = (
                    at[k * C:(k + 1) * C, k * C:(k + 1) * C].astype(BF16))
        return carry

    lax.fori_loop(0, ng // ilv, solve_groups, 0)

    nw = nw_ref[...]

    def step(c, states):
        r0 = pl.multiple_of(c * C, C)
        kcs = [k_s[pl.ds(r0, C), qh * HD:(qh + 1) * HD] for qh in range(nh)]
        qcs = [q_s[pl.ds(r0, C), qh * HD:(qh + 1) * HD] for qh in range(nh)]
        gccs = [gcc_s[h, pl.ds(r0, C), :] for h in heads]
        lhss = [jnp.concatenate([w_s[h, pl.ds(r0, C), :], (qcs[h // 2] * jnp.exp(gccs[h])).astype(BF16)], axis=0)
                for h in heads]
        wss = [_dot(lhss[h], states[h].astype(BF16)) for h in heads]
        vns = [(u_s[h, pl.ds(r0, C), :] - wss[h][:C]).astype(BF16) for h in heads]
        glast = [gccs[h][C - 1:C, :] for h in heads]
        kes = [(kcs[h // 2] * jnp.exp(glast[h] - gccs[h])).astype(BF16) for h in heads]
        outs = [wss[h][C:] + _dot(at_s[h, pl.ds(r0, C), :], vns[h]) for h in heads]
        new_states = [states[h] * jnp.exp(glast[h]) + _dot_tn(kes[h], vns[h]) for h in heads]
        for h in heads:
            zc = z_ref[pl.ds(r0, C), h * HD:(h + 1) * HD]
            o_ref[pl.ds(r0, C), h * HD:(h + 1) * HD] = (_rms_rows(outs[h]) * nw * _silu(zc)).astype(BF16)
        return tuple(new_states)

    s_fin = lax.fori_loop(0, nc, step, tuple(s0_ref[h] for h in heads))
    for h in heads:
        sout_ref[h] = s_fin[h]


def _gdn_core(p, prev8, conv_w_all, b_rows, a_rows, alog_b, dtb_b, norm_w_all, s0, layer,
              batch, t_len, t_valid):
    HD = GDN_HEAD
    ng8, gw = b_rows.shape[2], b_rows.shape[3]
    C = gw
    ng = t_len // gw
    alog_b = jnp.broadcast_to(alog_b[:, :, None, None], alog_b.shape + (1, gw))
    dtb_b = jnp.broadcast_to(dtb_b[:, :, None, None], dtb_b.shape + (1, gw))
    step_bytes = 104 * t_len * HD
    nh = GDN_MAX_HEADS_PER_STEP
    while nh > 1 and step_bytes * nh > GDN_STEP_VMEM_BYTES:
        nh //= 2
    ilv = math.gcd(ng, max(1, GDN_SOLVE_INTERLEAVE // nh))
    qw, vw, nv = nh * HD, 2 * nh * HD, 2 * nh
    kq = GDN_KEY_DIM // qw
    kz = GDN_CONV_DIM // vw
    vmem = step_bytes * nh + 16 * 2**20
    grid = (batch, GDN_QK_HEADS // nh)
    return pl.pallas_call(
        functools.partial(_gdn_body, t_len=t_len, t_valid=t_valid, gw=gw, ilv=ilv),
        grid=grid,
        in_specs=[
            pl.BlockSpec((t_len, qw), lambda b, h: (b, h)),
            pl.BlockSpec((t_len, qw), lambda b, h: (b, kq + h)),
            pl.BlockSpec((t_len, vw), lambda b, h: (b, kq + h)),
            pl.BlockSpec((t_len, vw), lambda b, h: (b, kz + h)),
            pl.BlockSpec((None, 8, qw), lambda b, h: (b, 0, h)),
            pl.BlockSpec((None, 8, qw), lambda b, h: (b, 0, kq + h)),
            pl.BlockSpec((None, 8, vw), lambda b, h: (b, 0, kq + h)),
            pl.BlockSpec((None, GDN_CONV, qw), lambda b, h: (layer, 0, h)),
            pl.BlockSpec((None, GDN_CONV, qw), lambda b, h: (layer, 0, kq + h)),
            pl.BlockSpec((None, GDN_CONV, vw), lambda b, h: (layer, 0, kq + h)),
            pl.BlockSpec((None, nv, ng8, gw), lambda b, h: (b, h, 0, 0)),
            pl.BlockSpec((None, nv, ng8, gw), lambda b, h: (b, h, 0, 0)),
            pl.BlockSpec((None, nv, 1, gw), lambda b, h: (layer, h, 0, 0)),
            pl.BlockSpec((None, nv, 1, gw), lambda b, h: (layer, h, 0, 0)),
            pl.BlockSpec((None, 1, HD), lambda b, h: (layer, 0, 0)),
            pl.BlockSpec((None, nv, HD, HD), lambda b, h: (b, h, 0, 0)),
        ],
        out_specs=[
            pl.BlockSpec((t_len, vw), lambda b, h: (b, h)),
            pl.BlockSpec((None, nv, HD, HD), lambda b, h: (b, h, 0, 0)),
        ],
        out_shape=[
            jax.ShapeDtypeStruct((batch * t_len, GDN_VAL_DIM), BF16),
            jax.ShapeDtypeStruct((batch, GDN_V_HEADS, HD, HD), F32),
        ],
        scratch_shapes=[
            pltpu.VMEM((t_len, qw), F32),
            pltpu.VMEM((t_len, qw), F32),
            pltpu.VMEM((t_len, vw), F32),
            pltpu.VMEM((nv, t_len, HD), F32),
            pltpu.VMEM((nv, t_len, HD), BF16),
            pltpu.VMEM((nv, t_len, C), BF16),
            pltpu.VMEM((nv, ng8, gw), F32),
            pltpu.VMEM((nv, t_len, HD), F32),
            pltpu.VMEM((nv, t_len, HD), F32),
        ],
        compiler_params=_cparams(("parallel", "arbitrary"), vmem),
        name="gdn_core",
    )(p, p, p, p, prev8, prev8, prev8, conv_w_all, conv_w_all, conv_w_all,
      b_rows, a_rows, alog_b, dtb_b, norm_w_all, s0)


def _diff_lambda(lq1_ref, lk1_ref, lq2_ref, lk2_ref, lam_init):
    s1 = jnp.sum(lq1_ref[...] * lk1_ref[...], axis=-1, keepdims=True)
    s2 = jnp.sum(lq2_ref[...] * lk2_ref[...], axis=-1, keepdims=True)
    return jnp.exp(s1) - jnp.exp(s2) + lam_init


def _flash_body(q_ref, k_ref, v_ref, lq1_ref, lk1_ref, lq2_ref, lk2_ref, sn_ref, o_ref,
                m_s, l_s, acc_s, *, tq, sub, lam_init):
    qi = pl.program_id(2)
    scale = DIFF_HEAD ** -0.5
    q = q_ref[...]
    q_maps = (q[:, :DIFF_HEAD].astype(BF16), q[:, DIFF_HEAD:].astype(BF16))
    m_s[...] = jnp.full(m_s.shape, -jnp.inf, F32)
    l_s[...] = jnp.zeros(l_s.shape, F32)
    acc_s[...] = jnp.zeros(acc_s.shape, F32)
    c2 = scale * math.log2(math.e)
    maps = (0, 1)

    def block(kb, masked):
        r0 = pl.multiple_of(kb * tq, tq)
        k = k_ref[pl.ds(r0, tq), :]
        v = v_ref[pl.ds(r0, tq), :].astype(BF16)
        ks = [k[:, c * DIFF_HEAD:(c + 1) * DIFF_HEAD].astype(BF16) for c in maps]
        chains = [(r, c) for r in range(tq // sub) for c in maps]

        def scores(r, c):
            return _dot_nt(q_maps[c][r * sub:(r + 1) * sub, :], ks[c])

        s_next = scores(*chains[0])
        for i, (r, c) in enumerate(chains):
            s = s_next
            if i + 1 < len(chains):
                s_next = scores(*chains[i + 1])
            rs = slice(r * sub, (r + 1) * sub)
            if masked:
                keep = (lax.broadcasted_iota(jnp.int32, (sub, tq), 1)
                        <= lax.broadcasted_iota(jnp.int32, (sub, tq), 0) + r * sub)
                s = jnp.where(keep, s, -jnp.inf)
            m_prev = m_s[c, rs, :]
            m_new = jnp.maximum(m_prev, jnp.max(s, axis=-1, keepdims=True))
            p = jnp.exp2((s - jnp.concatenate([m_new] * (tq // V7X_LANES), axis=1)) * c2)
            alpha = jnp.exp2((m_prev - m_new) * c2)
            p_lanes = p[:, :V7X_LANES]
            for t in range(1, tq // V7X_LANES):
                p_lanes = p_lanes + p[:, t * V7X_LANES:(t + 1) * V7X_LANES]
            l_s[c, rs, :] = alpha * l_s[c, rs, :] + p_lanes
            acc_s[c, rs, :] = (jnp.concatenate([alpha] * (DIFF_V_HEAD // V7X_LANES), axis=1) * acc_s[c, rs, :]
                               + _dot(p.astype(BF16), v))
            m_s[c, rs, :] = m_new

    def body(kb, carry):
        block(kb, False)
        return carry

    lax.fori_loop(0, qi, body, 0)
    block(qi, True)
    lam = _diff_lambda(lq1_ref, lk1_ref, lq2_ref, lk2_ref, lam_init)
    l0 = jnp.sum(l_s[0], axis=-1, keepdims=True)
    l1 = jnp.sum(l_s[1], axis=-1, keepdims=True)
    o = acc_s[0] / l0 - lam * (acc_s[1] / l1)
    o_ref[...] = (_rms_rows(o) * sn_ref[...] * (1.0 - lam_init)).astype(BF16)


def _flash(qk, v, lq1, lk1, lq2, lk2, sn_all, w_layer, lam_init, batch, t_len, tq):
    nq = t_len // tq
    dv = DIFF_V_HEAD
    vmem = 2 * tq * dv * 4 + 4 * t_len * dv * 4 + 2 * tq * dv * 2 + 2 * tq * dv * 4 + 8 * tq * tq * 4 + 4 * 2**20
    lam_spec = pl.BlockSpec((None, 1, DIFF_HEAD), lambda b, h, i: (w_layer, 0, 0))
    return pl.pallas_call(
        functools.partial(_flash_body, tq=tq, sub=min(FLASH_SUB_ROWS, tq), lam_init=lam_init),
        grid=(batch, DIFF_HEADS, nq),
        in_specs=[
            pl.BlockSpec((None, tq, dv), lambda b, h, i: (0, b * nq + i, h)),
            pl.BlockSpec((None, t_len, dv), lambda b, h, i: (1, b, h)),
            pl.BlockSpec((t_len, dv), lambda b, h, i: (b, h)),
            lam_spec, lam_spec, lam_spec, lam_spec,
            pl.BlockSpec((None, 1, dv), lambda b, h, i: (w_layer, 0, 0)),
        ],
        out_specs=pl.BlockSpec((tq, dv), lambda b, h, i: (b * nq + i, h)),
        out_shape=jax.ShapeDtypeStruct((batch * t_len, DIFF_V_DIM), BF16),
        scratch_shapes=[
            pltpu.VMEM((2, tq, V7X_LANES), F32),
            pltpu.VMEM((2, tq, V7X_LANES), F32),
            pltpu.VMEM((2, tq, dv), F32),
        ],
        compiler_params=_cparams(("parallel", "parallel", "arbitrary"), vmem),
        name="diff_flash",
    )(qk, qk, v, lq1, lk1, lq2, lk2, sn_all)


def _decode_body(pt_ref, q_ref, kn_ref, vn_ref, *rest, n_steps, pps, n_new, lam_init):
    del pt_ref
    kc_refs, vc_refs = rest[:pps], rest[pps:2 * pps]
    lq1_ref, lk1_ref, lq2_ref, lk2_ref, sn_ref, o_ref, m_s, l_s, acc_s = rest[2 * pps:]
    p = pl.program_id(1)
    scale = DIFF_HEAD ** -0.5
    n_half = q_ref.shape[1]
    n_rows = 2 * n_half
    n_cols = vn_ref.shape[0]

    @pl.when(p == 0)
    def _():
        m_s[...] = jnp.full(m_s.shape, -jnp.inf, F32)
        l_s[...] = jnp.zeros(l_s.shape, F32)
        acc_s[...] = jnp.zeros(acc_s.shape, F32)

    row_head = (lax.broadcasted_iota(jnp.int32, (n_rows, n_cols), 0) % n_half) // n_new
    col = lax.broadcasted_iota(jnp.int32, (n_rows, n_cols), 1)
    head_ok = row_head == col % DIFF_HEADS

    q_bf = [q_ref[c].astype(BF16) for c in range(2)]

    def update(k_refs, v_refs, mask):
        ss = [jnp.concatenate(
            [_dot_nt(q_bf[c], k_ref[pl.ds(c, n_cols, stride=2), :].astype(BF16)) for c in range(2)],
            axis=0) * scale for k_ref in k_refs]
        ss = [jnp.where(mask, s, -jnp.inf) for s in ss]
        m_prev = m_s[...]
        m_new = m_prev
        for s in ss:
            m_new = jnp.maximum(m_new, jnp.max(s, axis=-1, keepdims=True))
        alpha = jnp.exp(m_prev - m_new)
        prs = [jnp.exp(s - m_new) for s in ss]
        l_new = alpha * l_s[...]
        for pr in prs:
            l_new = l_new + jnp.sum(pr, axis=-1, keepdims=True)
        pvs = [_dot(pr.astype(BF16), v_ref[...].astype(BF16)) for pr, v_ref in zip(prs, v_refs)]
        acc = alpha * acc_s[...]
        for pv in pvs:
            acc = acc + pv
        l_s[...] = l_new
        m_s[...] = m_new
        acc_s[...] = acc

    update(kc_refs, vc_refs, head_ok)

    @pl.when(p == n_steps - 1)
    def _():
        qidx = lax.broadcasted_iota(jnp.int32, (n_rows, n_cols), 0) % n_new
        causal = col // DIFF_HEADS <= qidx
        update([kn_ref], [vn_ref], jnp.logical_and(head_ok, causal))
        lam = _diff_lambda(lq1_ref, lk1_ref, lq2_ref, lk2_ref, lam_init)
        o = acc_s[...] / l_s[...]
        od = o[:n_half] - lam * o[n_half:]
        o_ref[...] = _rms_rows(od) * sn_ref[...] * (1.0 - lam_init)


def _decode(page_table, q_maps, kn_pad, vn_pad, cache_k, cache_v, lq1, lk1, lq2, lk2, sn_all,
            w_layer, lam_init, n_new):
    bs, n_pages = page_table.shape
    n_half = q_maps.shape[2]
    k_rows, v_rows = cache_k.shape[2], cache_v.shape[2]
    pps = math.gcd(n_pages, DECODE_PAGES_PER_STEP)
    n_steps = n_pages // pps
    lam_spec = pl.BlockSpec((None, 1, DIFF_HEAD), lambda b, p, pt: (w_layer, 0, 0))
    vmem = 3 * pps * (k_rows * DIFF_HEAD + v_rows * DIFF_V_HEAD) * 4 + 20 * 2**20

    def page_spec(rows, width, i):
        return pl.BlockSpec((None, None, rows, width), lambda b, p, pt: (w_layer, pt[b, p * pps + i], 0, 0))

    grid_spec = pltpu.PrefetchScalarGridSpec(
        num_scalar_prefetch=1,
        grid=(bs, n_steps),
        in_specs=[
            pl.BlockSpec((None, 2, n_half, DIFF_HEAD), lambda b, p, pt: (b, 0, 0, 0)),
            pl.BlockSpec((None, k_rows, DIFF_HEAD), lambda b, p, pt: (b, 0, 0)),
            pl.BlockSpec((None, v_rows, DIFF_V_HEAD), lambda b, p, pt: (b, 0, 0)),
            *[page_spec(k_rows, DIFF_HEAD, i) for i in range(pps)],
            *[page_spec(v_rows, DIFF_V_HEAD, i) for i in range(pps)],
            lam_spec, lam_spec, lam_spec, lam_spec,
            pl.BlockSpec((None, 1, DIFF_V_HEAD), lambda b, p, pt: (w_layer, 0, 0)),
        ],
        out_specs=pl.BlockSpec((None, n_half, DIFF_V_HEAD), lambda b, p, pt: (b, 0, 0)),
        scratch_shapes=[
            pltpu.VMEM((2 * n_half, 1), F32),
            pltpu.VMEM((2 * n_half, 1), F32),
            pltpu.VMEM((2 * n_half, DIFF_V_HEAD), F32),
        ],
    )
    return pl.pallas_call(
        functools.partial(_decode_body, n_steps=n_steps, pps=pps, n_new=n_new, lam_init=lam_init),
        grid_spec=grid_spec,
        out_shape=jax.ShapeDtypeStruct((bs, n_half, DIFF_V_HEAD), F32),
        compiler_params=_cparams(("parallel", "arbitrary"), vmem),
        name="diff_decode",
    )(page_table, q_maps, kn_pad, vn_pad, *([cache_k] * pps), *([cache_v] * pps), lq1, lk1, lq2, lk2, sn_all)


def _rope_tables(pos):
    half = ROPE_DIM // 2
    inv = ROPE_THETA ** (-jnp.arange(half, dtype=F32) * 2.0 / ROPE_DIM)
    ang = pos.astype(F32)[:, None] * inv[None, :]
    cos, sin = jnp.cos(ang), jnp.sin(ang)
    n = pos.shape[0]
    tail = DIFF_HEAD - ROPE_DIM
    c = jnp.concatenate([cos, cos, jnp.ones((n, tail), F32)], axis=1)
    sa = jnp.concatenate([-sin, jnp.zeros((n, half + tail), F32)], axis=1)
    sb = jnp.concatenate([jnp.zeros((n, half), F32), sin, jnp.zeros((n, tail), F32)], axis=1)
    return c, sa, sb


def _gate_rows(ba, batch, t_len):
    gw = min(GDN_SOLVE_WIDTH, t_len)
    ng = t_len // gw
    ba = ba[:, :2 * GDN_V_HEADS]
    ba = ba.reshape(batch, ng, gw, 2, GDN_V_HEADS).transpose(3, 0, 4, 1, 2)
    ng8 = -(-ng // 8) * 8
    ba = jnp.pad(ba, ((0, 0), (0, 0), (0, 0), (0, ng8 - ng), (0, 0)))
    return ba[0], ba[1]


def kernel(x_prompt, x_sample, state_delta, state_conv, cache_k, cache_v, page_table, norm_ffn1, ffn1_w_gu, ffn1_w_dn, norm_mix, norm_ffn2, ffn2_w_gu, ffn2_w_dn, gdn_w_in, gdn_conv_w, gdn_a_log, gdn_dt_bias, gdn_norm_w, gdn_w_out, diff_w_in, diff_q_norm, diff_k_norm, diff_lam_q1, diff_lam_k1, diff_lam_q2, diff_lam_k2, diff_sub_norm, diff_w_out):
    bp, tp, d = x_prompt.shape
    bs, ts, _ = x_sample.shape
    depth = norm_ffn1.shape[0]
    n_pages = page_table.shape[1]
    past = n_pages * PAGE_SIZE
    C = GDN_CHUNK

    row = lambda w: w.reshape(w.shape[0], 1, w.shape[1])
    w_gu1, w_dn1, w_gu2, w_dn2 = ffn1_w_gu, ffn1_w_dn, ffn2_w_gu, ffn2_w_dn
    tn = 512
    tn_gdn = 1024
    gdn_n = GDN_CONV_DIM + GDN_VAL_DIM
    w_gin = gdn_w_in
    w_ba = jnp.pad(gdn_w_in[:, :, gdn_n:], ((0, 0), (0, 0), (0, V7X_LANES - 2 * GDN_V_HEADS)))
    w_gout, w_din, w_dout = gdn_w_out, diff_w_in, diff_w_out
    g_ffn1, g_mix, g_ffn2 = row(norm_ffn1), row(norm_mix), row(norm_ffn2)
    alog_b, dtb_b = gdn_a_log, gdn_dt_bias
    gdn_nw = row(gdn_norm_w)
    qk_n = jnp.stack([diff_q_norm, diff_k_norm], axis=1).reshape(-1, 1, DIFF_HEAD)
    lq1, lk1, lq2, lk2 = row(diff_lam_q1), row(diff_lam_k1), row(diff_lam_q2), row(diff_lam_k2)
    sub_n = row(diff_sub_norm)
    ck = cache_k.reshape(cache_k.shape[0], cache_k.shape[1], PAGE_SIZE * DIFF_HEADS * 2, DIFF_HEAD)
    cv = cache_v.reshape(cache_v.shape[0], cache_v.shape[1], PAGE_SIZE * DIFF_HEADS, DIFF_V_HEAD)

    rope_p = _rope_tables(jnp.arange(tp))
    rope_p = tuple(jnp.tile(t, (bp, 1)) for t in rope_p)
    rope_s = _rope_tables(past + jnp.arange(ts))
    rope_s = tuple(jnp.tile(t, (bs, 1)) for t in rope_s)

    xp = x_prompt.reshape(bp * tp, d)
    xs = x_sample.reshape(bs * ts, d)
    mp, ms = bp * tp, bs * ts
    tm_p, tm_big, tm_s = 1024, 1024, ms
    tf = 256
    ts_pad = -(-ts // C) * C

    prev8_p = jnp.zeros((bp, 8, GDN_CONV_DIM), F32)
    s0_p = jnp.zeros((bp, GDN_V_HEADS, GDN_HEAD, GDN_HEAD), F32)

    p_delta, p_conv, p_k, p_v = [], [], [], []
    s_delta, s_conv, s_k, s_v = [], [], [], []
    for i in range(depth):
        xp = _ffn(xp, g_ffn1, w_gu1, w_dn1, i, tm_p, tf)
        xs = _ffn(xs, g_ffn1, w_gu1, w_dn1, i, tm_s, tf)
        j = i // 2
        if i % 2 == 0:
            pp = _proj(xp, g_mix, w_gin, i, j, tm_big, tn_gdn, n_out=gdn_n)
            ps = _proj(xs, g_mix, w_gin, i, j, tm_s, tn_gdn, n_out=gdn_n)
            bap = _proj(xp, g_mix, w_ba, i, j, tm_big, V7X_LANES)
            bas = _proj(xs, g_mix, w_ba, i, j, tm_s, V7X_LANES)
            b_rows, a_rows = _gate_rows(bap, bp, tp)
            op, sp = _gdn_core(pp, prev8_p, gdn_conv_w, b_rows, a_rows, alog_b, dtb_b, gdn_nw, s0_p, j,
                               bp, tp, tp)
            p_conv.append(pp.reshape(bp, tp, gdn_n)[:, tp - (GDN_CONV - 1):, :GDN_CONV_DIM])
            p_delta.append(sp)
            xp = _outproj(op, w_gout, xp, j, tm_big, tn)

            ps3 = ps.reshape(bs, ts, gdn_n)
            ps_pad = jnp.pad(ps3, ((0, 0), (0, ts_pad - ts), (0, 0))).reshape(bs * ts_pad, gdn_n)
            prev8_s = jnp.pad(state_conv[j], ((0, 0), (8 - (GDN_CONV - 1), 0), (0, 0)))
            bas_pad = jnp.pad(bas.reshape(bs, ts, V7X_LANES), ((0, 0), (0, ts_pad - ts), (0, 0)))
            b_rows, a_rows = _gate_rows(bas_pad.reshape(bs * ts_pad, V7X_LANES), bs, ts_pad)
            os_, ss = _gdn_core(ps_pad, prev8_s, gdn_conv_w, b_rows, a_rows, alog_b, dtb_b, gdn_nw,
                                state_delta[j], j, bs, ts_pad, ts)
            xp_conv = jnp.concatenate([state_conv[j], ps3[:, :, :GDN_CONV_DIM]], axis=1)
            s_conv.append(xp_conv[:, ts:])
            s_delta.append(ss)
            os_ = os_.reshape(bs, ts_pad, GDN_VAL_DIM)[:, :ts].reshape(ms, GDN_VAL_DIM)
            xs = _outproj(os_, w_gout, xs, j, tm_s, tn)
        else:
            lam_init = 0.8 - 0.6 * math.exp(-0.3 * i)
            qkp = _qkproj(xp, g_mix, w_din, qk_n, *rope_p, i, j, tm_big, tn)
            vp = _proj(xp, g_mix, w_din, i, j, tm_big, tn, col0=2 * DIFF_QK_DIM, n_out=DIFF_V_DIM)
            qks = _qkproj(xs, g_mix, w_din, qk_n, *rope_s, i, j, tm_s, tn)
            vs = _proj(xs, g_mix, w_din, i, j, tm_s, tn, col0=2 * DIFF_QK_DIM, n_out=DIFF_V_DIM)
            qs, ks = qks[0], qks[1]
            op = _flash(qkp, vp, lq1, lk1, lq2, lk2, sub_n, j, lam_init, bp, tp, 512)
            xp = _outproj(op, w_dout, xp, j, tm_big, tn)
            p_k.append(qkp[1].reshape(bp, tp, DIFF_HEADS, 2, DIFF_HEAD))
            p_v.append(vp.reshape(bp, tp, DIFF_HEADS, DIFF_V_HEAD))

            q_maps = qs.reshape(bs, ts, DIFF_HEADS, 2, DIFF_HEAD).transpose(0, 3, 2, 1, 4)
            q_maps = q_maps.reshape(bs, 2, DIFF_HEADS * ts, DIFF_HEAD)
            kn_pad = jnp.pad(ks.reshape(bs, ts * DIFF_HEADS * 2, DIFF_HEAD),
                             ((0, 0), (0, (PAGE_SIZE - ts) * DIFF_HEADS * 2), (0, 0)))
            vn_pad = jnp.pad(vs.reshape(bs, ts * DIFF_HEADS, DIFF_V_HEAD),
                             ((0, 0), (0, (PAGE_SIZE - ts) * DIFF_HEADS), (0, 0)))
            od = _decode(page_table, q_maps, kn_pad, vn_pad, ck, cv, lq1, lk1, lq2, lk2, sub_n,
                         j, lam_init, ts)
            od = od.reshape(bs, DIFF_HEADS, ts, DIFF_V_HEAD).transpose(0, 2, 1, 3).reshape(ms, DIFF_V_DIM)
            xs = _outproj(od.astype(BF16), w_dout, xs, j, tm_s, tn)
            s_k.append(ks.reshape(bs, ts, DIFF_HEADS, 2, DIFF_HEAD))
            s_v.append(vs.reshape(bs, ts, DIFF_HEADS, DIFF_V_HEAD))
        xp = _ffn(xp, g_ffn2, w_gu2, w_dn2, i, tm_p, tf)
        xs = _ffn(xs, g_ffn2, w_gu2, w_dn2, i, tm_s, tf)

    return (xp.reshape(bp, tp, d), xs.reshape(bs, ts, d),
            jnp.stack(p_delta), jnp.stack(p_conv), jnp.stack(p_k), jnp.stack(p_v),
            jnp.stack(s_delta), jnp.stack(s_conv), jnp.stack(s_k), jnp.stack(s_v))
```

```python
import functools
import math

import jax
import jax.numpy as jnp
from jax import lax
from jax.experimental import pallas as pl
from jax.experimental.pallas import tpu as pltpu

F32 = jnp.float32
BF16 = jnp.bfloat16
EPS = 1e-6

D_MODEL = 2048
D_FF = 5632
GDN_QK_HEADS = 16
GDN_V_HEADS = 32
GDN_HEAD = 128
GDN_CONV = 4
GDN_CHUNK = 64
GDN_KEY_DIM = GDN_QK_HEADS * GDN_HEAD
GDN_VAL_DIM = GDN_V_HEADS * GDN_HEAD
GDN_CONV_DIM = 2 * GDN_KEY_DIM + GDN_VAL_DIM
GDN_IN_DIM = GDN_CONV_DIM + GDN_VAL_DIM + 2 * GDN_V_HEADS
DIFF_HEADS = 8
DIFF_HEAD = 128
DIFF_V_HEAD = 2 * DIFF_HEAD
DIFF_QK_DIM = DIFF_HEADS * 2 * DIFF_HEAD
DIFF_V_DIM = DIFF_HEADS * DIFF_V_HEAD
ROPE_DIM = DIFF_HEAD // 4
ROPE_THETA = 500000.0
PAGE_SIZE = 128

V7X_LANES = 128
V7X_VMEM_LIMIT_BYTES = 56 * 1024 * 1024

GDN_SOLVE_WIDTH = 128
GDN_SOLVE_INTERLEAVE = 4
GDN_MAX_HEADS_PER_STEP = 4
GDN_CONV_ROWS = 64
GDN_CONV_TILES_PER_TRIP = 32
GDN_STEP_VMEM_BYTES = 40 * 1024 * 1024
DECODE_PAGES_PER_STEP = 4
FLASH_SUB_ROWS = 256


def _cparams(semantics, vmem_bytes):
    return pltpu.CompilerParams(dimension_semantics=semantics,
                                vmem_limit_bytes=int(min(max(vmem_bytes, 16 * 2**20), V7X_VMEM_LIMIT_BYTES)))


def _dot(a, b):
    return jnp.dot(a, b, preferred_element_type=F32)


def _dot_nt(a, b):
    return lax.dot_general(a, b, (((1,), (1,)), ((), ())), preferred_element_type=F32)


def _dot_tn(a, b):
    return lax.dot_general(a, b, (((0,), (0,)), ((), ())), preferred_element_type=F32)


def _rms_rows(x):
    return x * lax.rsqrt(jnp.mean(x * x, axis=-1, keepdims=True) + EPS)


def _silu(x):
    return x * jax.nn.sigmoid(x)


def _lane_sumsq(x):
    ones = jnp.ones((x.shape[1], V7X_LANES), BF16)
    return _dot((x * x).astype(BF16), ones)


def _ffn_prologue(x_ref, g_ref, o_ref, xn_ref):
    @pl.when(pl.program_id(1) == 0)
    def _():
        x = x_ref[...]
        xn_ref[...] = (_rms_rows(x) * g_ref[...]).astype(BF16)
        o_ref[...] = x


def _ffn_body(x_ref, g_ref, wg_ref, wu_ref, wd_ref, o_ref, xn_ref):
    _ffn_prologue(x_ref, g_ref, o_ref, xn_ref)
    xn = xn_ref[...]
    gate = _dot(xn, wg_ref[...])
    up = _dot(xn, wu_ref[...])
    act = (_silu(gate) * up).astype(BF16)
    o_ref[...] += 0.5 * _dot(act, wd_ref[...])


def _ffn_cast_body(x_ref, g_ref, wg_ref, wu_ref, wd_ref, o_ref, wgb_ref, wub_ref, wdb_ref, xn_ref):
    _ffn_prologue(x_ref, g_ref, o_ref, xn_ref)
    wgb_ref[...] = wg_ref[...].astype(BF16)
    wub_ref[...] = wu_ref[...].astype(BF16)
    wdb_ref[...] = wd_ref[...].astype(BF16)
    xn = xn_ref[...]
    gate = _dot(xn, wgb_ref[...])
    up = _dot(xn, wub_ref[...])
    act = (_silu(gate) * up).astype(BF16)
    o_ref[...] += 0.5 * _dot(act, wdb_ref[...])


def _ffn_cast(x, g_all, wgu_all, wdn_all, layer, tf):
    m, d = x.shape
    f = wdn_all.shape[1]
    nf = f // tf
    vmem = 4 * m * d * 4 + 2 * 3 * d * tf * (4 + 2) + 6 * m * tf * 4 + 3 * d * tf * (4 + 2)
    return pl.pallas_call(
        _ffn_cast_body,
        grid=(1, nf),
        in_specs=[
            pl.BlockSpec((m, d), lambda i, j: (0, 0)),
            pl.BlockSpec((None, 1, d), lambda i, j: (layer, 0, 0)),
            pl.BlockSpec((None, d, tf), lambda i, j: (layer, 0, j)),
            pl.BlockSpec((None, d, tf), lambda i, j: (layer, 0, nf + j)),
            pl.BlockSpec((None, tf, d), lambda i, j: (layer, j, 0)),
        ],
        out_specs=[
            pl.BlockSpec((m, d), lambda i, j: (0, 0)),
            pl.BlockSpec((d, tf), lambda i, j: (0, j)),
            pl.BlockSpec((d, tf), lambda i, j: (0, j)),
            pl.BlockSpec((tf, d), lambda i, j: (j, 0)),
        ],
        out_shape=[
            jax.ShapeDtypeStruct((m, d), F32),
            jax.ShapeDtypeStruct((d, f), BF16),
            jax.ShapeDtypeStruct((d, f), BF16),
            jax.ShapeDtypeStruct((f, d), BF16),
        ],
        scratch_shapes=[pltpu.VMEM((m, d), BF16)],
        compiler_params=_cparams(("arbitrary", "arbitrary"), vmem),
        name="half_ffn_cast",
    )(x, g_all, wgu_all, wgu_all, wdn_all)


def _ffn(x, g_all, wg, wu, wd, layer, tm, tf):
    m, d = x.shape
    f = wd.shape[0]
    vmem = 4 * tm * d * 4 + tm * d * 2 + 2 * 3 * d * tf * 2 + 6 * tm * tf * 4
    return pl.pallas_call(
        _ffn_body,
        grid=(m // tm, f // tf),
        in_specs=[
            pl.BlockSpec((tm, d), lambda i, j: (i, 0)),
            pl.BlockSpec((None, 1, d), lambda i, j: (layer, 0, 0)),
            pl.BlockSpec((d, tf), lambda i, j: (0, j)),
            pl.BlockSpec((d, tf), lambda i, j: (0, j)),
            pl.BlockSpec((tf, d), lambda i, j: (j, 0)),
        ],
        out_specs=pl.BlockSpec((tm, d), lambda i, j: (i, 0)),
        out_shape=jax.ShapeDtypeStruct((m, d), F32),
        scratch_shapes=[pltpu.VMEM((tm, d), BF16)],
        compiler_params=_cparams(("parallel", "arbitrary"), vmem),
        name="half_ffn",
    )(x, g_all, wg, wu, wd)


def _proj_body(x_ref, g_ref, w_ref, o_ref, xn_ref):
    @pl.when(pl.program_id(1) == 0)
    def _():
        xn_ref[...] = (_rms_rows(x_ref[...]) * g_ref[...]).astype(BF16)

    o_ref[...] = _dot(xn_ref[...], w_ref[...].astype(BF16))


def _proj_t_body(x_ref, g_ref, wt_ref, o_ref, xn_ref):
    @pl.when(pl.program_id(1) == 0)
    def _():
        xn_ref[...] = (_rms_rows(x_ref[...]) * g_ref[...]).astype(BF16)

    o_ref[...] = _dot_nt(xn_ref[...], wt_ref[...].astype(BF16))


def _proj(x, g_all, w_all, layer, w_layer, tm, tn, col0=0, n_out=None, transposed=False):
    m, d = x.shape
    n_all = w_all.shape[1] if transposed else w_all.shape[2]
    n_out = n_all - col0 if n_out is None else n_out
    jb = col0 // tn
    vmem = 4 * tm * d * 4 + tm * d * 2 + 2 * d * tn * w_all.dtype.itemsize + 4 * tm * tn * 4 + d * tn * 2
    if transposed:
        w_spec = pl.BlockSpec((None, tn, d), lambda i, j: (w_layer, jb + j, 0))
    else:
        w_spec = pl.BlockSpec((None, d, tn), lambda i, j: (w_layer, 0, jb + j))
    return pl.pallas_call(
        _proj_t_body if transposed else _proj_body,
        grid=(m // tm, n_out // tn),
        in_specs=[
            pl.BlockSpec((tm, d), lambda i, j: (i, 0)),
            pl.BlockSpec((None, 1, d), lambda i, j: (layer, 0, 0)),
            w_spec,
        ],
        out_specs=pl.BlockSpec((tm, tn), lambda i, j: (i, j)),
        out_shape=jax.ShapeDtypeStruct((m, n_out), F32),
        scratch_shapes=[pltpu.VMEM((tm, d), BF16)],
        compiler_params=_cparams(("parallel", "arbitrary"), vmem),
        name="norm_proj",
    )(x, g_all, w_all)


def _qkproj_body(x_ref, g_ref, w_ref, nw_ref, cos_ref, sa_ref, sb_ref, o_ref, xn_ref):
    @pl.when(pl.program_id(1) == 0)
    def _():
        xn_ref[...] = (_rms_rows(x_ref[...]) * g_ref[...]).astype(BF16)

    xn = xn_ref[...]
    wn = nw_ref[...]
    cos, sa, sb = cos_ref[...], sa_ref[...], sb_ref[...]
    pair = 2 * DIFF_HEAD
    n_pairs = o_ref.shape[1] // pair

    def mm(g):
        return _dot(xn, w_ref[:, g * pair:(g + 1) * pair].astype(BF16))

    acc_next = mm(0)
    for g in range(n_pairs):
        acc = acc_next
        if g + 1 < n_pairs:
            acc_next = mm(g + 1)
        for h in range(2):
            y = acc[:, h * DIFF_HEAD:(h + 1) * DIFF_HEAD]
            y = y * lax.rsqrt(_lane_sumsq(y) * (1.0 / DIFF_HEAD) + EPS) * wn
            y = (y * cos + pltpu.roll(y, DIFF_HEAD - ROPE_DIM // 2, 1) * sa
                 + pltpu.roll(y, ROPE_DIM // 2, 1) * sb)
            c0 = g * pair + h * DIFF_HEAD
            o_ref[:, c0:c0 + DIFF_HEAD] = y


def _qkproj(x, g_all, w_all, qk_norm, cos, sa, sb, layer, w_layer, tm, tn):
    m, d = x.shape
    nq = DIFF_QK_DIM // tn
    vmem = (4 * tm * d * 4 + tm * d * 2 + 2 * d * tn * w_all.dtype.itemsize + 6 * tm * tn * 4
            + 6 * tm * 128 * 4 + d * tn * 2)
    return pl.pallas_call(
        _qkproj_body,
        grid=(m // tm, 2 * nq),
        in_specs=[
            pl.BlockSpec((tm, d), lambda i, j: (i, 0)),
            pl.BlockSpec((None, 1, d), lambda i, j: (layer, 0, 0)),
            pl.BlockSpec((None, d, tn), lambda i, j: (w_layer, 0, j)),
            pl.BlockSpec((None, 1, DIFF_HEAD), lambda i, j: (2 * w_layer + j // nq, 0, 0)),
            pl.BlockSpec((tm, DIFF_HEAD), lambda i, j: (i, 0)),
            pl.BlockSpec((tm, DIFF_HEAD), lambda i, j: (i, 0)),
            pl.BlockSpec((tm, DIFF_HEAD), lambda i, j: (i, 0)),
        ],
        out_specs=pl.BlockSpec((None, tm, tn), lambda i, j: (j // nq, i, j % nq)),
        out_shape=jax.ShapeDtypeStruct((2, m, DIFF_QK_DIM), F32),
        scratch_shapes=[pltpu.VMEM((tm, d), BF16)],
        compiler_params=_cparams(("parallel", "arbitrary"), vmem),
        name="diff_qk_proj",
    )(x, g_all, w_all, qk_norm, cos, sa, sb)


def _outproj_body(y_ref, w_ref, x_ref, o_ref):
    o_ref[...] = x_ref[...] + _dot(y_ref[...], w_ref[...].astype(BF16))


def _outproj(y, w_all, x, w_layer, tm, tn):
    m, k = y.shape
    d = x.shape[1]
    vmem = 2 * tm * k * 2 + 2 * k * tn * w_all.dtype.itemsize + 6 * tm * tn * 4 + k * tn * 2
    return pl.pallas_call(
        _outproj_body,
        grid=(m // tm, d // tn),
        in_specs=[
            pl.BlockSpec((tm, k), lambda i, j: (i, 0)),
            pl.BlockSpec((None, k, tn), lambda i, j: (w_layer, 0, j)),
            pl.BlockSpec((tm, tn), lambda i, j: (i, j)),
        ],
        out_specs=pl.BlockSpec((tm, tn), lambda i, j: (i, j)),
        out_shape=jax.ShapeDtypeStruct((m, d), F32),
        compiler_params=_cparams(("parallel", "arbitrary"), vmem),
        name="out_proj",
    )(y, w_all, x)


def _unit_lower_inverse_minus_identity(mms, level_masks, base_mask):
    ys = [-(mm * base_mask) for mm in mms]
    for mask in level_masks:
        os_ = [mm * mask for mm in mms]
        y_bf = [y.astype(BF16) for y in ys]
        zs = [o + _dot(yb, o.astype(BF16)) for o, yb in zip(os_, y_bf)]
        ys = [y - z - _dot(z.astype(BF16), yb) for y, z, yb in zip(ys, zs, y_bf)]
    return ys


def _gdn_body(qp_ref, kp_ref, vp_ref, z_ref, q8_ref, k8_ref, v8_ref, cwq_ref, cwk_ref, cwv_ref,
              b_ref, a_ref, alog_ref, dtb_ref, nw_ref, s0_ref,
              o_ref, sout_ref,
              q_s, k_s, v_s, u_s, w_s, at_s, grow_s, gcc_s, bc_s,
              *, t_len, t_valid, gw, ilv):
    C = gw
    HD = GDN_HEAD
    nc = t_len // C
    ng = t_len // gw
    cpg = gw // C
    nh = q_s.shape[1] // HD
    heads = tuple(range(2 * nh))
    rt = min(GDN_CONV_ROWS, t_len)

    def conv_block(x_ref, p8_ref, cw_ref, r0, cs, first):
        head = p8_ref[:, cs] if first else x_ref[pl.ds(r0 - 8, 8), cs]
        cur = x_ref[pl.ds(r0, rt), cs]
        ext = jnp.concatenate([head, cur], axis=0)
        w = cw_ref[:, cs]
        y = ext[5:5 + rt, :] * w[0:1, :]
        y = y + ext[6:6 + rt, :] * w[1:2, :]
        y = y + ext[7:7 + rt, :] * w[2:3, :]
        y = y + cur * w[3:4, :]
        y = _silu(y)
        if t_valid < t_len:
            rows = r0 + lax.broadcasted_iota(jnp.int32, y.shape, 0)
            y = jnp.where(rows < t_valid, y, 0.0)
        return y

    def l2n(x):
        return x * lax.rsqrt(jnp.sum(x * x, axis=-1, keepdims=True) + EPS)

    def conv_rows(r0, first):
        for qh in range(nh):
            cs = slice(qh * HD, (qh + 1) * HD)
            q_s[pl.ds(r0, rt), cs] = l2n(conv_block(qp_ref, q8_ref, cwq_ref, r0, cs, first)) * (HD ** -0.5)
            k_s[pl.ds(r0, rt), cs] = l2n(conv_block(kp_ref, k8_ref, cwk_ref, r0, cs, first))
        for h in heads:
            cs = slice(h * HD, (h + 1) * HD)
            v_s[pl.ds(r0, rt), cs] = conv_block(vp_ref, v8_ref, cwv_ref, r0, cs, first)

    per_trip = math.gcd(t_len // rt, GDN_CONV_TILES_PER_TRIP)
    conv_rows(0, True)
    for s in range(1, per_trip):
        conv_rows(s * rt, False)

    def conv_body(r, carry):
        for s in range(per_trip):
            conv_rows(pl.multiple_of((r * per_trip + s) * rt, rt), False)
        return carry

    lax.fori_loop(1, t_len // (rt * per_trip), conv_body, 0)

    ri = lax.broadcasted_iota(jnp.int32, (gw, gw), 0)
    ci = lax.broadcasted_iota(jnp.int32, (gw, gw), 1)
    chunk_shift = C.bit_length() - 1
    same_chunk = (ri >> chunk_shift) == (ci >> chunk_shift)
    lower = jnp.logical_and(same_chunk, ci <= ri)
    strict = jnp.logical_and(same_chunk, ci < ri)
    upper_ones = jnp.where(jnp.logical_and(same_chunk, ri <= ci), 1.0, 0.0)
    eye = jnp.where(ri == ci, 1.0, 0.0)
    hi = lax.Precision.HIGHEST
    for h in heads:
        beta = jax.nn.sigmoid(b_ref[h])
        g = -jnp.exp(alog_ref[h]) * jax.nn.softplus(a_ref[h] + dtb_ref[h])
        if t_valid < t_len:
            pos = (lax.broadcasted_iota(jnp.int32, g.shape, 0) * gw
                   + lax.broadcasted_iota(jnp.int32, g.shape, 1))
            g = jnp.where(pos < t_valid, g, 0.0)
            beta = jnp.where(pos < t_valid, beta, 0.0)
        gc = jnp.dot(g, upper_ones, precision=hi, preferred_element_type=F32)
        grow_s[h] = gc
        gc_t = lax.dot_general(eye, gc, (((1,), (1,)), ((), ())), precision=hi,
                               preferred_element_type=F32)
        beta_t = lax.dot_general(eye, beta, (((1,), (1,)), ((), ())), precision=hi,
                                 preferred_element_type=F32)
        for i in range(ng):
            gcc_s[h, i * gw:(i + 1) * gw, :] = jnp.broadcast_to(gc_t[:, i:i + 1], (gw, HD))
            bc_s[h, i * gw:(i + 1) * gw, :] = jnp.broadcast_to(beta_t[:, i:i + 1], (gw, HD))

    base_mask = jnp.where((ri >> 1) == (ci >> 1), 1.0, 0.0)
    level_masks = []
    for s in range(1, chunk_shift):
        same_big = (ri >> (s + 1)) == (ci >> (s + 1))
        same_small = (ri >> s) == (ci >> s)
        level_masks.append(jnp.where(same_big, 1.0, 0.0) - jnp.where(same_small, 1.0, 0.0))

    def lane_tile(x):
        if gw <= HD:
            return x[:, :gw]
        return jnp.concatenate([x] * (gw // HD), axis=1)

    def solve_groups(it, carry):
        chains = []
        for gg in range(ilv):
            gi = it * ilv + gg
            r0 = pl.multiple_of(gi * gw, gw)
            for qh in range(nh):
                kc = k_s[pl.ds(r0, gw), qh * HD:(qh + 1) * HD]
                k_bf = kc.astype(BF16)
                qk = _dot_nt(q_s[pl.ds(r0, gw), qh * HD:(qh + 1) * HD].astype(BF16), k_bf)
                for h in (2 * qh, 2 * qh + 1):
                    gcc = gcc_s[h, pl.ds(r0, gw), :]
                    gd = lane_tile(gcc) - grow_s[h, pl.ds(gi, 1), :]
                    dec = jnp.where(lower, jnp.exp(jnp.where(lower, gd, 0.0)), 0.0)
                    beta_c = bc_s[h, pl.ds(r0, gw), :]
                    chains.append((h, r0, kc * beta_c, k_bf, dec, gcc, beta_c, qk))
        mms = [jnp.where(strict, _dot_nt(kb.astype(BF16), k_bf) * dec, 0.0)
               for (_, _, kb, k_bf, dec, _, _, _) in chains]
        ys = _unit_lower_inverse_minus_identity(mms, level_masks, base_mask)
        rhss = [jnp.concatenate([v_s[pl.ds(r0, gw), h * HD:(h + 1) * HD] * beta_c, kb * jnp.exp(gcc)], axis=1)
                for (h, r0, kb, _, _, gcc, beta_c, _) in chains]
        uws = [rhs + _dot(y.astype(BF16), rhs.astype(BF16)) for rhs, y in zip(rhss, ys)]
        for (h, r0, _, _, dec, _, _, qk), uw in zip(chains, uws):
            u_s[h, pl.ds(r0, gw), :] = uw[:, :HD]
            w_s[h, pl.ds(r0, gw), :] = uw[:, HD:].astype(BF16)
            at_s[h, pl.ds(r0, gw), :] = (qk * dec).astype(BF16)
        return carry

    lax.fori_loop(0, ng // ilv, solve_groups, 0)

    nw = nw_ref[...]

    def step(c, states):
        r0 = pl.multiple_of(c * C, C)
        kcs = [k_s[pl.ds(r0, C), qh * HD:(qh + 1) * HD] for qh in range(nh)]
        qcs = [q_s[pl.ds(r0, C), qh * HD:(qh + 1) * HD] for qh in range(nh)]
        gccs = [gcc_s[h, pl.ds(r0, C), :] for h in heads]
        lhss = [jnp.concatenate([w_s[h, pl.ds(r0, C), :], (qcs[h // 2] * jnp.exp(gccs[h])).astype(BF16)], axis=0)
                for h in heads]
        wss = [_dot(lhss[h], states[h].astype(BF16)) for h in heads]
        vns = [(u_s[h, pl.ds(r0, C), :] - wss[h][:C]).astype(BF16) for h in heads]
        glast = [gccs[h][C - 1:C, :] for h in heads]
        kes = [(kcs[h // 2] * jnp.exp(glast[h] - gccs[h])).astype(BF16) for h in heads]
        outs = [wss[h][C:] + _dot(at_s[h, pl.ds(r0, C), :], vns[h]) for h in heads]
        new_states = [states[h] * jnp.exp(glast[h]) + _dot_tn(kes[h], vns[h]) for h in heads]
        for h in heads:
            zc = z_ref[pl.ds(r0, C), h * HD:(h + 1) * HD]
            o_ref[pl.ds(r0, C), h * HD:(h + 1) * HD] = (_rms_rows(outs[h]) * nw * _silu(zc)).astype(BF16)
        return tuple(new_states)

    s_fin = lax.fori_loop(0, nc, step, tuple(s0_ref[h] for h in heads))
    for h in heads:
        sout_ref[h] = s_fin[h]


def _gdn_core(p, prev8, conv_w_all, b_rows, a_rows, alog_b, dtb_b, norm_w_all, s0, layer,
              batch, t_len, t_valid):
    HD = GDN_HEAD
    ng8, gw = b_rows.shape[2], b_rows.shape[3]
    C = gw
    ng = t_len // gw
    alog_b = jnp.broadcast_to(alog_b[:, :, None, None], alog_b.shape + (1, gw))
    dtb_b = jnp.broadcast_to(dtb_b[:, :, None, None], dtb_b.shape + (1, gw))
    step_bytes = 104 * t_len * HD
    nh = GDN_MAX_HEADS_PER_STEP
    while nh > 1 and step_bytes * nh > GDN_STEP_VMEM_BYTES:
        nh //= 2
    ilv = math.gcd(ng, max(1, GDN_SOLVE_INTERLEAVE // nh))
    qw, vw, nv = nh * HD, 2 * nh * HD, 2 * nh
    kq = GDN_KEY_DIM // qw
    kz = GDN_CONV_DIM // vw
    vmem = step_bytes * nh + 16 * 2**20
    grid = (batch, GDN_QK_HEADS // nh)
    return pl.pallas_call(
        functools.partial(_gdn_body, t_len=t_len, t_valid=t_valid, gw=gw, ilv=ilv),
        grid=grid,
        in_specs=[
            pl.BlockSpec((t_len, qw), lambda b, h: (b, h)),
            pl.BlockSpec((t_len, qw), lambda b, h: (b, kq + h)),
            pl.BlockSpec((t_len, vw), lambda b, h: (b, kq + h)),
            pl.BlockSpec((t_len, vw), lambda b, h: (b, kz + h)),
            pl.BlockSpec((None, 8, qw), lambda b, h: (b, 0, h)),
            pl.BlockSpec((None, 8, qw), lambda b, h: (b, 0, kq + h)),
            pl.BlockSpec((None, 8, vw), lambda b, h: (b, 0, kq + h)),
            pl.BlockSpec((None, GDN_CONV, qw), lambda b, h: (layer, 0, h)),
            pl.BlockSpec((None, GDN_CONV, qw), lambda b, h: (layer, 0, kq + h)),
            pl.BlockSpec((None, GDN_CONV, vw), lambda b, h: (layer, 0, kq + h)),
            pl.BlockSpec((None, nv, ng8, gw), lambda b, h: (b, h, 0, 0)),
            pl.BlockSpec((None, nv, ng8, gw), lambda b, h: (b, h, 0, 0)),
            pl.BlockSpec((None, nv, 1, gw), lambda b, h: (layer, h, 0, 0)),
            pl.BlockSpec((None, nv, 1, gw), lambda b, h: (layer, h, 0, 0)),
            pl.BlockSpec((None, 1, HD), lambda b, h: (layer, 0, 0)),
            pl.BlockSpec((None, nv, HD, HD), lambda b, h: (b, h, 0, 0)),
        ],
        out_specs=[
            pl.BlockSpec((t_len, vw), lambda b, h: (b, h)),
            pl.BlockSpec((None, nv, HD, HD), lambda b, h: (b, h, 0, 0)),
        ],
        out_shape=[
            jax.ShapeDtypeStruct((batch * t_len, GDN_VAL_DIM), BF16),
            jax.ShapeDtypeStruct((batch, GDN_V_HEADS, HD, HD), F32),
        ],
        scratch_shapes=[
            pltpu.VMEM((t_len, qw), F32),
            pltpu.VMEM((t_len, qw), F32),
            pltpu.VMEM((t_len, vw), F32),
            pltpu.VMEM((nv, t_len, HD), F32),
            pltpu.VMEM((nv, t_len, HD), BF16),
            pltpu.VMEM((nv, t_len, C), BF16),
            pltpu.VMEM((nv, ng8, gw), F32),
            pltpu.VMEM((nv, t_len, HD), F32),
            pltpu.VMEM((nv, t_len, HD), F32),
        ],
        compiler_params=_cparams(("parallel", "arbitrary"), vmem),
        name="gdn_core",
    )(p, p, p, p, prev8, prev8, prev8, conv_w_all, conv_w_all, conv_w_all,
      b_rows, a_rows, alog_b, dtb_b, norm_w_all, s0)


def _diff_lambda(lq1_ref, lk1_ref, lq2_ref, lk2_ref, lam_init):
    s1 = jnp.sum(lq1_ref[...] * lk1_ref[...], axis=-1, keepdims=True)
    s2 = jnp.sum(lq2_ref[...] * lk2_ref[...], axis=-1, keepdims=True)
    return jnp.exp(s1) - jnp.exp(s2) + lam_init


def _flash_body(q_ref, k_ref, v_ref, lq1_ref, lk1_ref, lq2_ref, lk2_ref, sn_ref, o_ref,
                m_s, l_s, acc_s, *, tq, sub, lam_init):
    qi = pl.program_id(2)
    scale = DIFF_HEAD ** -0.5
    q = q_ref[...]
    q_maps = (q[:, :DIFF_HEAD].astype(BF16), q[:, DIFF_HEAD:].astype(BF16))
    m_s[...] = jnp.full(m_s.shape, -jnp.inf, F32)
    l_s[...] = jnp.zeros(l_s.shape, F32)
    acc_s[...] = jnp.zeros(acc_s.shape, F32)
    c2 = scale * math.log2(math.e)
    maps = (0, 1)

    def block(kb, masked):
        r0 = pl.multiple_of(kb * tq, tq)
        k = k_ref[pl.ds(r0, tq), :]
        v = v_ref[pl.ds(r0, tq), :].astype(BF16)
        ks = [k[:, c * DIFF_HEAD:(c + 1) * DIFF_HEAD].astype(BF16) for c in maps]
        chains = [(r, c) for r in range(tq // sub) for c in maps]

        def scores(r, c):
            return _dot_nt(q_maps[c][r * sub:(r + 1) * sub, :], ks[c])

        s_next = scores(*chains[0])
        for i, (r, c) in enumerate(chains):
            s = s_next
            if i + 1 < len(chains):
                s_next = scores(*chains[i + 1])
            rs = slice(r * sub, (r + 1) * sub)
            if masked:
                keep = (lax.broadcasted_iota(jnp.int32, (sub, tq), 1)
                        <= lax.broadcasted_iota(jnp.int32, (sub, tq), 0) + r * sub)
                s = jnp.where(keep, s, -jnp.inf)
            m_prev = m_s[c, rs, :]
            m_new = jnp.maximum(m_prev, jnp.max(s, axis=-1, keepdims=True))
            p = jnp.exp2((s - jnp.concatenate([m_new] * (tq // V7X_LANES), axis=1)) * c2)
            alpha = jnp.exp2((m_prev - m_new) * c2)
            p_lanes = p[:, :V7X_LANES]
            for t in range(1, tq // V7X_LANES):
                p_lanes = p_lanes + p[:, t * V7X_LANES:(t + 1) * V7X_LANES]
            l_s[c, rs, :] = alpha * l_s[c, rs, :] + p_lanes
            acc_s[c, rs, :] = (jnp.concatenate([alpha] * (DIFF_V_HEAD // V7X_LANES), axis=1) * acc_s[c, rs, :]
                               + _dot(p.astype(BF16), v))
            m_s[c, rs, :] = m_new

    def body(kb, carry):
        block(kb, False)
        return carry

    lax.fori_loop(0, qi, body, 0)
    block(qi, True)
    lam = _diff_lambda(lq1_ref, lk1_ref, lq2_ref, lk2_ref, lam_init)
    l0 = jnp.sum(l_s[0], axis=-1, keepdims=True)
    l1 = jnp.sum(l_s[1], axis=-1, keepdims=True)
    o = acc_s[0] / l0 - lam * (acc_s[1] / l1)
    o_ref[...] = (_rms_rows(o) * sn_ref[...] * (1.0 - lam_init)).astype(BF16)


def _flash(qk, v, lq1, lk1, lq2, lk2, sn_all, w_layer, lam_init, batch, t_len, tq):
    nq = t_len // tq
    dv = DIFF_V_HEAD
    vmem = 2 * tq * dv * 4 + 4 * t_len * dv * 4 + 2 * tq * dv * 2 + 2 * tq * dv * 4 + 8 * tq * tq * 4 + 4 * 2**20
    lam_spec = pl.BlockSpec((None, 1, DIFF_HEAD), lambda b, h, i: (w_layer, 0, 0))
    return pl.pallas_call(
        functools.partial(_flash_body, tq=tq, sub=min(FLASH_SUB_ROWS, tq), lam_init=lam_init),
        grid=(batch, DIFF_HEADS, nq),
        in_specs=[
            pl.BlockSpec((None, tq, dv), lambda b, h, i: (0, b * nq + i, h)),
            pl.BlockSpec((None, t_len, dv), lambda b, h, i: (1, b, h)),
            pl.BlockSpec((t_len, dv), lambda b, h, i: (b, h)),
            lam_spec, lam_spec, lam_spec, lam_spec,
            pl.BlockSpec((None, 1, dv), lambda b, h, i: (w_layer, 0, 0)),
        ],
        out_specs=pl.BlockSpec((tq, dv), lambda b, h, i: (b * nq + i, h)),
        out_shape=jax.ShapeDtypeStruct((batch * t_len, DIFF_V_DIM), BF16),
        scratch_shapes=[
            pltpu.VMEM((2, tq, V7X_LANES), F32),
            pltpu.VMEM((2, tq, V7X_LANES), F32),
            pltpu.VMEM((2, tq, dv), F32),
        ],
        compiler_params=_cparams(("parallel", "parallel", "arbitrary"), vmem),
        name="diff_flash",
    )(qk, qk, v, lq1, lk1, lq2, lk2, sn_all)


def _decode_body(pt_ref, q_ref, kn_ref, vn_ref, *rest, n_steps, pps, n_new, lam_init):
    del pt_ref
    kc_refs, vc_refs = rest[:pps], rest[pps:2 * pps]
    lq1_ref, lk1_ref, lq2_ref, lk2_ref, sn_ref, o_ref, m_s, l_s, acc_s = rest[2 * pps:]
    p = pl.program_id(1)
    scale = DIFF_HEAD ** -0.5
    n_half = q_ref.shape[1]
    n_rows = 2 * n_half
    n_cols = vn_ref.shape[0]

    @pl.when(p == 0)
    def _():
        m_s[...] = jnp.full(m_s.shape, -jnp.inf, F32)
        l_s[...] = jnp.zeros(l_s.shape, F32)
        acc_s[...] = jnp.zeros(acc_s.shape, F32)

    row_head = (lax.broadcasted_iota(jnp.int32, (n_rows, n_cols), 0) % n_half) // n_new
    col = lax.broadcasted_iota(jnp.int32, (n_rows, n_cols), 1)
    head_ok = row_head == col % DIFF_HEADS

    q_bf = [q_ref[c].astype(BF16) for c in range(2)]

    def update(k_refs, v_refs, mask):
        ss = [jnp.concatenate(
            [_dot_nt(q_bf[c], k_ref[pl.ds(c, n_cols, stride=2), :].astype(BF16)) for c in range(2)],
            axis=0) * scale for k_ref in k_refs]
        ss = [jnp.where(mask, s, -jnp.inf) for s in ss]
        m_prev = m_s[...]
        m_new = m_prev
        for s in ss:
            m_new = jnp.maximum(m_new, jnp.max(s, axis=-1, keepdims=True))
        alpha = jnp.exp(m_prev - m_new)
        prs = [jnp.exp(s - m_new) for s in ss]
        l_new = alpha * l_s[...]
        for pr in prs:
            l_new = l_new + jnp.sum(pr, axis=-1, keepdims=True)
        pvs = [_dot(pr.astype(BF16), v_ref[...].astype(BF16)) for pr, v_ref in zip(prs, v_refs)]
        acc = alpha * acc_s[...]
        for pv in pvs:
            acc = acc + pv
        l_s[...] = l_new
        m_s[...] = m_new
        acc_s[...] = acc

    update(kc_refs, vc_refs, head_ok)

    @pl.when(p == n_steps - 1)
    def _():
        qidx = lax.broadcasted_iota(jnp.int32, (n_rows, n_cols), 0) % n_new
        causal = col // DIFF_HEADS <= qidx
        update([kn_ref], [vn_ref], jnp.logical_and(head_ok, causal))
        lam = _diff_lambda(lq1_ref, lk1_ref, lq2_ref, lk2_ref, lam_init)
        o = acc_s[...] / l_s[...]
        od = o[:n_half] - lam * o[n_half:]
        o_ref[...] = _rms_rows(od) * sn_ref[...] * (1.0 - lam_init)


def _decode(page_table, q_maps, kn_pad, vn_pad, cache_k, cache_v, lq1, lk1, lq2, lk2, sn_all,
            w_layer, lam_init, n_new):
    bs, n_pages = page_table.shape
    n_half = q_maps.shape[2]
    k_rows, v_rows = cache_k.shape[2], cache_v.shape[2]
    pps = math.gcd(n_pages, DECODE_PAGES_PER_STEP)
    n_steps = n_pages // pps
    lam_spec = pl.BlockSpec((None, 1, DIFF_HEAD), lambda b, p, pt: (w_layer, 0, 0))
    vmem = 3 * pps * (k_rows * DIFF_HEAD + v_rows * DIFF_V_HEAD) * 4 + 20 * 2**20

    def page_spec(rows, width, i):
        return pl.BlockSpec((None, None, rows, width), lambda b, p, pt: (w_layer, pt[b, p * pps + i], 0, 0))

    grid_spec = pltpu.PrefetchScalarGridSpec(
        num_scalar_prefetch=1,
        grid=(bs, n_steps),
        in_specs=[
            pl.BlockSpec((None, 2, n_half, DIFF_HEAD), lambda b, p, pt: (b, 0, 0, 0)),
            pl.BlockSpec((None, k_rows, DIFF_HEAD), lambda b, p, pt: (b, 0, 0)),
            pl.BlockSpec((None, v_rows, DIFF_V_HEAD), lambda b, p, pt: (b, 0, 0)),
            *[page_spec(k_rows, DIFF_HEAD, i) for i in range(pps)],
            *[page_spec(v_rows, DIFF_V_HEAD, i) for i in range(pps)],
            lam_spec, lam_spec, lam_spec, lam_spec,
            pl.BlockSpec((None, 1, DIFF_V_HEAD), lambda b, p, pt: (w_layer, 0, 0)),
        ],
        out_specs=pl.BlockSpec((None, n_half, DIFF_V_HEAD), lambda b, p, pt: (b, 0, 0)),
        scratch_shapes=[
            pltpu.VMEM((2 * n_half, 1), F32),
            pltpu.VMEM((2 * n_half, 1), F32),
            pltpu.VMEM((2 * n_half, DIFF_V_HEAD), F32),
        ],
    )
    return pl.pallas_call(
        functools.partial(_decode_body, n_steps=n_steps, pps=pps, n_new=n_new, lam_init=lam_init),
        grid_spec=grid_spec,
        out_shape=jax.ShapeDtypeStruct((bs, n_half, DIFF_V_HEAD), F32),
        compiler_params=_cparams(("parallel", "arbitrary"), vmem),
        name="diff_decode",
    )(page_table, q_maps, kn_pad, vn_pad, *([cache_k] * pps), *([cache_v] * pps), lq1, lk1, lq2, lk2, sn_all)


def _rope_tables(pos):
    half = ROPE_DIM // 2
    inv = ROPE_THETA ** (-jnp.arange(half, dtype=F32) * 2.0 / ROPE_DIM)
    ang = pos.astype(F32)[:, None] * inv[None, :]
    cos, sin = jnp.cos(ang), jnp.sin(ang)
    n = pos.shape[0]
    tail = DIFF_HEAD - ROPE_DIM
    c = jnp.concatenate([cos, cos, jnp.ones((n, tail), F32)], axis=1)
    sa = jnp.concatenate([-sin, jnp.zeros((n, half + tail), F32)], axis=1)
    sb = jnp.concatenate([jnp.zeros((n, half), F32), sin, jnp.zeros((n, tail), F32)], axis=1)
    return c, sa, sb


def _gate_rows(ba, batch, t_len):
    gw = min(GDN_SOLVE_WIDTH, t_len)
    ng = t_len // gw
    ba = ba[:, :2 * GDN_V_HEADS]
    ba = ba.reshape(batch, ng, gw, 2, GDN_V_HEADS).transpose(3, 0, 4, 1, 2)
    ng8 = -(-ng // 8) * 8
    ba = jnp.pad(ba, ((0, 0), (0, 0), (0, 0), (0, ng8 - ng), (0, 0)))
    return ba[0], ba[1]


def kernel(x_prompt, x_sample, state_delta, state_conv, cache_k, cache_v, page_table, norm_ffn1, ffn1_w_gu, ffn1_w_dn, norm_mix, norm_ffn2, ffn2_w_gu, ffn2_w_dn, gdn_w_in, gdn_conv_w, gdn_a_log, gdn_dt_bias, gdn_norm_w, gdn_w_out, diff_w_in, diff_q_norm, diff_k_norm, diff_lam_q1, diff_lam_k1, diff_lam_q2, diff_lam_k2, diff_sub_norm, diff_w_out):
    bp, tp, d = x_prompt.shape
    bs, ts, _ = x_sample.shape
    depth = norm_ffn1.shape[0]
    n_pages = page_table.shape[1]
    past = n_pages * PAGE_SIZE
    C = GDN_CHUNK

    row = lambda w: w.reshape(w.shape[0], 1, w.shape[1])
    w_gu1, w_dn1, w_gu2, w_dn2 = ffn1_w_gu, ffn1_w_dn, ffn2_w_gu, ffn2_w_dn
    tn = 512
    tn_gdn = 1024
    gdn_n = GDN_CONV_DIM + GDN_VAL_DIM
    n_ba = 2 * GDN_V_HEADS
    w_gin_t = jnp.swapaxes(gdn_w_in, 1, 2)
    w_gout, w_din, w_dout = gdn_w_out, diff_w_in, diff_w_out
    g_ffn1, g_mix, g_ffn2 = row(norm_ffn1), row(norm_mix), row(norm_ffn2)
    alog_b, dtb_b = gdn_a_log, gdn_dt_bias
    gdn_nw = row(gdn_norm_w)
    qk_n = jnp.stack([diff_q_norm, diff_k_norm], axis=1).reshape(-1, 1, DIFF_HEAD)
    lq1, lk1, lq2, lk2 = row(diff_lam_q1), row(diff_lam_k1), row(diff_lam_q2), row(diff_lam_k2)
    sub_n = row(diff_sub_norm)
    ck = cache_k.reshape(cache_k.shape[0], cache_k.shape[1], PAGE_SIZE * DIFF_HEADS * 2, DIFF_HEAD)
    cv = cache_v.reshape(cache_v.shape[0], cache_v.shape[1], PAGE_SIZE * DIFF_HEADS, DIFF_V_HEAD)

    rope_p = _rope_tables(jnp.arange(tp))
    rope_p = tuple(jnp.tile(t, (bp, 1)) for t in rope_p)
    rope_s = _rope_tables(past + jnp.arange(ts))
    rope_s = tuple(jnp.tile(t, (bs, 1)) for t in rope_s)

    xp = x_prompt.reshape(bp * tp, d)
    xs = x_sample.reshape(bs * ts, d)
    mp, ms = bp * tp, bs * ts
    tm_p, tm_big, tm_s = 1024, 1024, ms
    tf, tf_cast = 512, 256
    ts_pad = -(-ts // C) * C

    prev8_p = jnp.zeros((bp, 8, GDN_CONV_DIM), F32)
    s0_p = jnp.zeros((bp, GDN_V_HEADS, GDN_HEAD, GDN_HEAD), F32)

    p_delta, p_conv, p_k, p_v = [], [], [], []
    s_delta, s_conv, s_k, s_v = [], [], [], []
    for i in range(depth):
        xs, wg, wu, wd = _ffn_cast(xs, g_ffn1, w_gu1, w_dn1, i, tf_cast)
        xp = _ffn(xp, g_ffn1, wg, wu, wd, i, tm_p, tf)
        j = i // 2
        if i % 2 == 0:
            pp = _proj(xp, g_mix, w_gin_t, i, j, tm_big, tn_gdn, n_out=gdn_n, transposed=True)
            ps = _proj(xs, g_mix, w_gin_t, i, j, tm_s, tn_gdn, n_out=gdn_n, transposed=True)
            bap = _proj(xp, g_mix, w_gin_t, i, j, tm_big, n_ba, col0=gdn_n, n_out=n_ba, transposed=True)
            bas = _proj(xs, g_mix, w_gin_t, i, j, tm_s, n_ba, col0=gdn_n, n_out=n_ba, transposed=True)
            b_rows, a_rows = _gate_rows(bap, bp, tp)
            op, sp = _gdn_core(pp, prev8_p, gdn_conv_w, b_rows, a_rows, alog_b, dtb_b, gdn_nw, s0_p, j,
                               bp, tp, tp)
            p_conv.append(pp.reshape(bp, tp, gdn_n)[:, tp - (GDN_CONV - 1):, :GDN_CONV_DIM])
            p_delta.append(sp)
            xp = _outproj(op, w_gout, xp, j, tm_big, tn)

            ps3 = ps.reshape(bs, ts, gdn_n)
            ps_pad = jnp.pad(ps3, ((0, 0), (0, ts_pad - ts), (0, 0))).reshape(bs * ts_pad, gdn_n)
            prev8_s = jnp.pad(state_conv[j], ((0, 0), (8 - (GDN_CONV - 1), 0), (0, 0)))
            bas_pad = jnp.pad(bas.reshape(bs, ts, n_ba), ((0, 0), (0, ts_pad - ts), (0, 0)))
            b_rows, a_rows = _gate_rows(bas_pad.reshape(bs * ts_pad, n_ba), bs, ts_pad)
            os_, ss = _gdn_core(ps_pad, prev8_s, gdn_conv_w, b_rows, a_rows, alog_b, dtb_b, gdn_nw,
                                state_delta[j], j, bs, ts_pad, ts)
            xp_conv = jnp.concatenate([state_conv[j], ps3[:, :, :GDN_CONV_DIM]], axis=1)
            s_conv.append(xp_conv[:, ts:])
            s_delta.append(ss)
            os_ = os_.reshape(bs, ts_pad, GDN_VAL_DIM)[:, :ts].reshape(ms, GDN_VAL_DIM)
            xs = _outproj(os_, w_gout, xs, j, tm_s, tn)
        else:
            lam_init = 0.8 - 0.6 * math.exp(-0.3 * i)
            qkp = _qkproj(xp, g_mix, w_din, qk_n, *rope_p, i, j, tm_big, tn)
            vp = _proj(xp, g_mix, w_din, i, j, tm_big, tn, col0=2 * DIFF_QK_DIM, n_out=DIFF_V_DIM)
            qks = _qkproj(xs, g_mix, w_din, qk_n, *rope_s, i, j, tm_s, tn)
            vs = _proj(xs, g_mix, w_din, i, j, tm_s, tn, col0=2 * DIFF_QK_DIM, n_out=DIFF_V_DIM)
            qs, ks = qks[0], qks[1]
            op = _flash(qkp, vp, lq1, lk1, lq2, lk2, sub_n, j, lam_init, bp, tp, 512)
            xp = _outproj(op, w_dout, xp, j, tm_big, tn)
            p_k.append(qkp[1].reshape(bp, tp, DIFF_HEADS, 2, DIFF_HEAD))
            p_v.append(vp.reshape(bp, tp, DIFF_HEADS, DIFF_V_HEAD))

            q_maps = qs.reshape(bs, ts, DIFF_HEADS, 2, DIFF_HEAD).transpose(0, 3, 2, 1, 4)
            q_maps = q_maps.reshape(bs, 2, DIFF_HEADS * ts, DIFF_HEAD)
            kn_pad = jnp.pad(ks.reshape(bs, ts * DIFF_HEADS * 2, DIFF_HEAD),
                             ((0, 0), (0, (PAGE_SIZE - ts) * DIFF_HEADS * 2), (0, 0)))
            vn_pad = jnp.pad(vs.reshape(bs, ts * DIFF_HEADS, DIFF_V_HEAD),
                             ((0, 0), (0, (PAGE_SIZE - ts) * DIFF_HEADS), (0, 0)))
            od = _decode(page_table, q_maps, kn_pad, vn_pad, ck, cv, lq1, lk1, lq2, lk2, sub_n,
                         j, lam_init, ts)
            od = od.reshape(bs, DIFF_HEADS, ts, DIFF_V_HEAD).transpose(0, 2, 1, 3).reshape(ms, DIFF_V_DIM)
            xs = _outproj(od.astype(BF16), w_dout, xs, j, tm_s, tn)
            s_k.append(ks.reshape(bs, ts, DIFF_HEADS, 2, DIFF_HEAD))
            s_v.append(vs.reshape(bs, ts, DIFF_HEADS, DIFF_V_HEAD))
        xs, wg, wu, wd = _ffn_cast(xs, g_ffn2, w_gu2, w_dn2, i, tf_cast)
        xp = _ffn(xp, g_ffn2, wg, wu, wd, i, tm_p, tf)

    return (xp.reshape(bp, tp, d), xs.reshape(bs, ts, d),
            jnp.stack(p_delta), jnp.stack(p_conv), jnp.stack(p_k), jnp.stack(p_v),
            jnp.stack(s_delta), jnp.stack(s_conv), jnp.stack(s_k), jnp.stack(s_v))
```

```python
import functools
import math

import jax
import jax.numpy as jnp
from jax import lax
from jax.experimental import pallas as pl
from jax.experimental.pallas import tpu as pltpu

F32 = jnp.float32
BF16 = jnp.bfloat16
EPS = 1e-6

D_MODEL = 2048
D_FF = 5632
GDN_QK_HEADS = 16
GDN_V_HEADS = 32
GDN_HEAD = 128
GDN_CONV = 4
GDN_CHUNK = 64
GDN_KEY_DIM = GDN_QK_HEADS * GDN_HEAD
GDN_VAL_DIM = GDN_V_HEADS * GDN_HEAD
GDN_CONV_DIM = 2 * GDN_KEY_DIM + GDN_VAL_DIM
GDN_IN_DIM = GDN_CONV_DIM + GDN_VAL_DIM + 2 * GDN_V_HEADS
DIFF_HEADS = 8
DIFF_HEAD = 128
DIFF_V_HEAD = 2 * DIFF_HEAD
DIFF_QK_DIM = DIFF_HEADS * 2 * DIFF_HEAD
DIFF_V_DIM = DIFF_HEADS * DIFF_V_HEAD
ROPE_DIM = DIFF_HEAD // 4
ROPE_THETA = 500000.0
PAGE_SIZE = 128

V7X_LANES = 128
V7X_VMEM_LIMIT_BYTES = 56 * 1024 * 1024

GDN_SOLVE_WIDTH = 128
GDN_SOLVE_INTERLEAVE = 4
GDN_MAX_HEADS_PER_STEP = 4
GDN_CONV_ROWS = 64
GDN_CONV_TILES_PER_TRIP = 32
GDN_STEP_VMEM_BYTES = 44 * 1024 * 1024
DECODE_PAGES_PER_STEP = 4
FLASH_SUB_ROWS = 256


def _cparams(semantics, vmem_bytes):
    return pltpu.CompilerParams(dimension_semantics=semantics,
                                vmem_limit_bytes=int(min(max(vmem_bytes, 16 * 2**20), V7X_VMEM_LIMIT_BYTES)))


def _dot(a, b):
    return jnp.dot(a, b, preferred_element_type=F32)


def _dot_nt(a, b):
    return lax.dot_general(a, b, (((1,), (1,)), ((), ())), preferred_element_type=F32)


def _dot_tn(a, b):
    return lax.dot_general(a, b, (((0,), (0,)), ((), ())), preferred_element_type=F32)


def _rms_rows(x):
    return x * lax.rsqrt(jnp.mean(x * x, axis=-1, keepdims=True) + EPS)


def _silu(x):
    return x * jax.nn.sigmoid(x)


def _lane_sumsq(x):
    ones = jnp.ones((x.shape[1], V7X_LANES), BF16)
    return _dot((x * x).astype(BF16), ones)


def _ffn_prologue(x_ref, g_ref, o_ref, xn_ref):
    @pl.when(pl.program_id(1) == 0)
    def _():
        x = x_ref[...]
        xn_ref[...] = (_rms_rows(x) * g_ref[...]).astype(BF16)
        o_ref[...] = x


def _ffn_body(x_ref, g_ref, wg_ref, wu_ref, wd_ref, o_ref, xn_ref):
    _ffn_prologue(x_ref, g_ref, o_ref, xn_ref)
    xn = xn_ref[...]
    gate = _dot(xn, wg_ref[...])
    up = _dot(xn, wu_ref[...])
    act = (_silu(gate) * up).astype(BF16)
    o_ref[...] += 0.5 * _dot(act, wd_ref[...])


def _ffn_cast_body(x_ref, g_ref, wg_ref, wu_ref, wd_ref, o_ref, wgb_ref, wub_ref, wdb_ref, xn_ref):
    _ffn_prologue(x_ref, g_ref, o_ref, xn_ref)
    wgb_ref[...] = wg_ref[...].astype(BF16)
    wub_ref[...] = wu_ref[...].astype(BF16)
    wdb_ref[...] = wd_ref[...].astype(BF16)
    xn = xn_ref[...]
    gate = _dot(xn, wgb_ref[...])
    up = _dot(xn, wub_ref[...])
    act = (_silu(gate) * up).astype(BF16)
    o_ref[...] += 0.5 * _dot(act, wdb_ref[...])


def _ffn_cast(x, g_all, wgu_all, wdn_all, layer, tf):
    m, d = x.shape
    f = wdn_all.shape[1]
    nf = f // tf
    vmem = 4 * m * d * 4 + 2 * 3 * d * tf * (4 + 2) + 6 * m * tf * 4 + 3 * d * tf * (4 + 2)
    return pl.pallas_call(
        _ffn_cast_body,
        grid=(1, nf),
        in_specs=[
            pl.BlockSpec((m, d), lambda i, j: (0, 0)),
            pl.BlockSpec((None, 1, d), lambda i, j: (layer, 0, 0)),
            pl.BlockSpec((None, d, tf), lambda i, j: (layer, 0, j)),
            pl.BlockSpec((None, d, tf), lambda i, j: (layer, 0, nf + j)),
            pl.BlockSpec((None, tf, d), lambda i, j: (layer, j, 0)),
        ],
        out_specs=[
            pl.BlockSpec((m, d), lambda i, j: (0, 0)),
            pl.BlockSpec((d, tf), lambda i, j: (0, j)),
            pl.BlockSpec((d, tf), lambda i, j: (0, j)),
            pl.BlockSpec((tf, d), lambda i, j: (j, 0)),
        ],
        out_shape=[
            jax.ShapeDtypeStruct((m, d), F32),
            jax.ShapeDtypeStruct((d, f), BF16),
            jax.ShapeDtypeStruct((d, f), BF16),
            jax.ShapeDtypeStruct((f, d), BF16),
        ],
        scratch_shapes=[pltpu.VMEM((m, d), BF16)],
        compiler_params=_cparams(("arbitrary", "arbitrary"), vmem),
        name="half_ffn_cast",
    )(x, g_all, wgu_all, wgu_all, wdn_all)


def _ffn(x, g_all, wg, wu, wd, layer, tm, tf):
    m, d = x.shape
    f = wd.shape[0]
    vmem = 4 * tm * d * 4 + tm * d * 2 + 2 * 3 * d * tf * 2 + 6 * tm * tf * 4
    return pl.pallas_call(
        _ffn_body,
        grid=(m // tm, f // tf),
        in_specs=[
            pl.BlockSpec((tm, d), lambda i, j: (i, 0)),
            pl.BlockSpec((None, 1, d), lambda i, j: (layer, 0, 0)),
            pl.BlockSpec((d, tf), lambda i, j: (0, j)),
            pl.BlockSpec((d, tf), lambda i, j: (0, j)),
            pl.BlockSpec((tf, d), lambda i, j: (j, 0)),
        ],
        out_specs=pl.BlockSpec((tm, d), lambda i, j: (i, 0)),
        out_shape=jax.ShapeDtypeStruct((m, d), F32),
        scratch_shapes=[pltpu.VMEM((tm, d), BF16)],
        compiler_params=_cparams(("parallel", "arbitrary"), vmem),
        name="half_ffn",
    )(x, g_all, wg, wu, wd)


def _proj_body(x_ref, g_ref, w_ref, o_ref, xn_ref):
    @pl.when(pl.program_id(1) == 0)
    def _():
        xn_ref[...] = (_rms_rows(x_ref[...]) * g_ref[...]).astype(BF16)

    o_ref[...] = _dot(xn_ref[...], w_ref[...].astype(BF16)).astype(o_ref.dtype)


def _proj_t_body(x_ref, g_ref, wt_ref, o_ref, xn_ref):
    @pl.when(pl.program_id(1) == 0)
    def _():
        xn_ref[...] = (_rms_rows(x_ref[...]) * g_ref[...]).astype(BF16)

    o_ref[...] = _dot_nt(xn_ref[...], wt_ref[...].astype(BF16)).astype(o_ref.dtype)


def _proj(x, g_all, w_all, layer, w_layer, tm, tn, col0=0, n_out=None, transposed=False, out_dtype=F32):
    m, d = x.shape
    n_all = w_all.shape[1] if transposed else w_all.shape[2]
    n_out = n_all - col0 if n_out is None else n_out
    jb = col0 // tn
    vmem = 4 * tm * d * 4 + tm * d * 2 + 2 * d * tn * w_all.dtype.itemsize + 4 * tm * tn * 4 + d * tn * 2
    if transposed:
        w_spec = pl.BlockSpec((None, tn, d), lambda i, j: (w_layer, jb + j, 0))
    else:
        w_spec = pl.BlockSpec((None, d, tn), lambda i, j: (w_layer, 0, jb + j))
    return pl.pallas_call(
        _proj_t_body if transposed else _proj_body,
        grid=(m // tm, n_out // tn),
        in_specs=[
            pl.BlockSpec((tm, d), lambda i, j: (i, 0)),
            pl.BlockSpec((None, 1, d), lambda i, j: (layer, 0, 0)),
            w_spec,
        ],
        out_specs=pl.BlockSpec((tm, tn), lambda i, j: (i, j)),
        out_shape=jax.ShapeDtypeStruct((m, n_out), out_dtype),
        scratch_shapes=[pltpu.VMEM((tm, d), BF16)],
        compiler_params=_cparams(("parallel", "arbitrary"), vmem),
        name="norm_proj",
    )(x, g_all, w_all)


def _qkproj_body(x_ref, g_ref, w_ref, nw_ref, cos_ref, sa_ref, sb_ref, o_ref, xn_ref):
    @pl.when(pl.program_id(1) == 0)
    def _():
        xn_ref[...] = (_rms_rows(x_ref[...]) * g_ref[...]).astype(BF16)

    xn = xn_ref[...]
    wn = nw_ref[...]
    cos, sa, sb = cos_ref[...], sa_ref[...], sb_ref[...]
    pair = 2 * DIFF_HEAD
    n_pairs = o_ref.shape[1] // pair

    def mm(g):
        return _dot(xn, w_ref[:, g * pair:(g + 1) * pair].astype(BF16))

    acc_next = mm(0)
    for g in range(n_pairs):
        acc = acc_next
        if g + 1 < n_pairs:
            acc_next = mm(g + 1)
        for h in range(2):
            y = acc[:, h * DIFF_HEAD:(h + 1) * DIFF_HEAD]
            y = y * lax.rsqrt(_lane_sumsq(y) * (1.0 / DIFF_HEAD) + EPS) * wn
            y = (y * cos + pltpu.roll(y, DIFF_HEAD - ROPE_DIM // 2, 1) * sa
                 + pltpu.roll(y, ROPE_DIM // 2, 1) * sb)
            c0 = g * pair + h * DIFF_HEAD
            o_ref[:, c0:c0 + DIFF_HEAD] = y


def _qkproj(x, g_all, w_all, qk_norm, cos, sa, sb, layer, w_layer, tm, tn):
    m, d = x.shape
    nq = DIFF_QK_DIM // tn
    vmem = (4 * tm * d * 4 + tm * d * 2 + 2 * d * tn * w_all.dtype.itemsize + 6 * tm * tn * 4
            + 6 * tm * 128 * 4 + d * tn * 2)
    return pl.pallas_call(
        _qkproj_body,
        grid=(m // tm, 2 * nq),
        in_specs=[
            pl.BlockSpec((tm, d), lambda i, j: (i, 0)),
            pl.BlockSpec((None, 1, d), lambda i, j: (layer, 0, 0)),
            pl.BlockSpec((None, d, tn), lambda i, j: (w_layer, 0, j)),
            pl.BlockSpec((None, 1, DIFF_HEAD), lambda i, j: (2 * w_layer + j // nq, 0, 0)),
            pl.BlockSpec((tm, DIFF_HEAD), lambda i, j: (i, 0)),
            pl.BlockSpec((tm, DIFF_HEAD), lambda i, j: (i, 0)),
            pl.BlockSpec((tm, DIFF_HEAD), lambda i, j: (i, 0)),
        ],
        out_specs=pl.BlockSpec((None, tm, tn), lambda i, j: (j // nq, i, j % nq)),
        out_shape=jax.ShapeDtypeStruct((2, m, DIFF_QK_DIM), F32),
        scratch_shapes=[pltpu.VMEM((tm, d), BF16)],
        compiler_params=_cparams(("parallel", "arbitrary"), vmem),
        name="diff_qk_proj",
    )(x, g_all, w_all, qk_norm, cos, sa, sb)


def _outproj_body(y_ref, w_ref, x_ref, o_ref):
    o_ref[...] = x_ref[...] + _dot(y_ref[...], w_ref[...].astype(BF16))


def _outproj(y, w_all, x, w_layer, tm, tn):
    m, k = y.shape
    d = x.shape[1]
    vmem = 2 * tm * k * 2 + 2 * k * tn * w_all.dtype.itemsize + 6 * tm * tn * 4 + k * tn * 2
    return pl.pallas_call(
        _outproj_body,
        grid=(m // tm, d // tn),
        in_specs=[
            pl.BlockSpec((tm, k), lambda i, j: (i, 0)),
            pl.BlockSpec((None, k, tn), lambda i, j: (w_layer, 0, j)),
            pl.BlockSpec((tm, tn), lambda i, j: (i, j)),
        ],
        out_specs=pl.BlockSpec((tm, tn), lambda i, j: (i, j)),
        out_shape=jax.ShapeDtypeStruct((m, d), F32),
        compiler_params=_cparams(("parallel", "arbitrary"), vmem),
        name="out_proj",
    )(y, w_all, x)


def _unit_lower_inverse_minus_identity(mms, level_masks, base_mask):
    ys = [-(mm * base_mask) for mm in mms]
    for mask in level_masks:
        os_ = [mm * mask for mm in mms]
        y_bf = [y.astype(BF16) for y in ys]
        zs = [o + _dot(yb, o.astype(BF16)) for o, yb in zip(os_, y_bf)]
        ys = [y - z - _dot(z.astype(BF16), yb) for y, z, yb in zip(ys, zs, y_bf)]
    return ys


def _gdn_body(qp_ref, kp_ref, vp_ref, z_ref, q8_ref, k8_ref, v8_ref, cwq_ref, cwk_ref, cwv_ref,
              b_ref, a_ref, alog_ref, dtb_ref, nw_ref, s0_ref,
              o_ref, sout_ref,
              q_s, k_s, v_s, u_s, w_s, at_s, grow_s, gcc_s, bc_s,
              *, t_len, t_valid, gw, ilv):
    C = gw
    HD = GDN_HEAD
    nc = t_len // C
    ng = t_len // gw
    cpg = gw // C
    nh = q_s.shape[1] // HD
    heads = tuple(range(2 * nh))
    rt = min(GDN_CONV_ROWS, t_len)

    def conv_block(x_ref, p8_ref, cw_ref, r0, cs, first):
        head = p8_ref[:, cs] if first else x_ref[pl.ds(r0 - 16, 16), cs].astype(F32)[8:, :]
        cur = x_ref[pl.ds(r0, rt), cs].astype(F32)
        ext = jnp.concatenate([head, cur], axis=0)
        w = cw_ref[:, cs]
        y = ext[5:5 + rt, :] * w[0:1, :]
        y = y + ext[6:6 + rt, :] * w[1:2, :]
        y = y + ext[7:7 + rt, :] * w[2:3, :]
        y = y + cur * w[3:4, :]
        y = _silu(y)
        if t_valid < t_len:
            rows = r0 + lax.broadcasted_iota(jnp.int32, y.shape, 0)
            y = jnp.where(rows < t_valid, y, 0.0)
        return y

    def l2n(x):
        return x * lax.rsqrt(jnp.sum(x * x, axis=-1, keepdims=True) + EPS)

    def conv_rows(r0, first):
        for qh in range(nh):
            cs = slice(qh * HD, (qh + 1) * HD)
            q_s[pl.ds(r0, rt), cs] = l2n(conv_block(qp_ref, q8_ref, cwq_ref, r0, cs, first)) * (HD ** -0.5)
            k_s[pl.ds(r0, rt), cs] = l2n(conv_block(kp_ref, k8_ref, cwk_ref, r0, cs, first))
        for h in heads:
            cs = slice(h * HD, (h + 1) * HD)
            v_s[pl.ds(r0, rt), cs] = conv_block(vp_ref, v8_ref, cwv_ref, r0, cs, first)

    per_trip = math.gcd(t_len // rt, GDN_CONV_TILES_PER_TRIP)
    conv_rows(0, True)
    for s in range(1, per_trip):
        conv_rows(s * rt, False)

    def conv_body(r, carry):
        for s in range(per_trip):
            conv_rows(pl.multiple_of((r * per_trip + s) * rt, rt), False)
        return carry

    lax.fori_loop(1, t_len // (rt * per_trip), conv_body, 0)

    ri = lax.broadcasted_iota(jnp.int32, (gw, gw), 0)
    ci = lax.broadcasted_iota(jnp.int32, (gw, gw), 1)
    chunk_shift = C.bit_length() - 1
    same_chunk = (ri >> chunk_shift) == (ci >> chunk_shift)
    lower = jnp.logical_and(same_chunk, ci <= ri)
    strict = jnp.logical_and(same_chunk, ci < ri)
    upper_ones = jnp.where(jnp.logical_and(same_chunk, ri <= ci), 1.0, 0.0)
    eye = jnp.where(ri == ci, 1.0, 0.0)
    hi = lax.Precision.HIGHEST
    for h in heads:
        beta = jax.nn.sigmoid(b_ref[h])
        g = -jnp.exp(alog_ref[h]) * jax.nn.softplus(a_ref[h] + dtb_ref[h])
        if t_valid < t_len:
            pos = (lax.broadcasted_iota(jnp.int32, g.shape, 0) * gw
                   + lax.broadcasted_iota(jnp.int32, g.shape, 1))
            g = jnp.where(pos < t_valid, g, 0.0)
            beta = jnp.where(pos < t_valid, beta, 0.0)
        gc = jnp.dot(g, upper_ones, precision=hi, preferred_element_type=F32)
        grow_s[h] = gc
        gc_t = lax.dot_general(eye, gc, (((1,), (1,)), ((), ())), precision=hi,
                               preferred_element_type=F32)
        beta_t = lax.dot_general(eye, beta, (((1,), (1,)), ((), ())), precision=hi,
                                 preferred_element_type=F32)
        for i in range(ng):
            gcc_s[h, i * gw:(i + 1) * gw, :] = jnp.broadcast_to(gc_t[:, i:i + 1], (gw, HD))
            bc_s[h, i * gw:(i + 1) * gw, :] = jnp.broadcast_to(beta_t[:, i:i + 1], (gw, HD))

    base_mask = jnp.where((ri >> 1) == (ci >> 1), 1.0, 0.0)
    level_masks = []
    for s in range(1, chunk_shift):
        same_big = (ri >> (s + 1)) == (ci >> (s + 1))
        same_small = (ri >> s) == (ci >> s)
        level_masks.append(jnp.where(same_big, 1.0, 0.0) - jnp.where(same_small, 1.0, 0.0))

    def lane_tile(x):
        if gw <= HD:
            return x[:, :gw]
        return jnp.concatenate([x] * (gw // HD), axis=1)

    def solve_groups(it, carry):
        chains = []
        for gg in range(ilv):
            gi = it * ilv + gg
            r0 = pl.multiple_of(gi * gw, gw)
            for qh in range(nh):
                kc = k_s[pl.ds(r0, gw), qh * HD:(qh + 1) * HD]
                k_bf = kc.astype(BF16)
                qk = _dot_nt(q_s[pl.ds(r0, gw), qh * HD:(qh + 1) * HD].astype(BF16), k_bf)
                for h in (2 * qh, 2 * qh + 1):
                    gcc = gcc_s[h, pl.ds(r0, gw), :]
                    gd = lane_tile(gcc) - grow_s[h, pl.ds(gi, 1), :]
                    dec = jnp.where(lower, jnp.exp(jnp.where(lower, gd, 0.0)), 0.0)
                    beta_c = bc_s[h, pl.ds(r0, gw), :]
                    chains.append((h, r0, kc * beta_c, k_bf, dec, gcc, beta_c, qk))
        mms = [jnp.where(strict, _dot_nt(kb.astype(BF16), k_bf) * dec, 0.0)
               for (_, _, kb, k_bf, dec, _, _, _) in chains]
        ys = _unit_lower_inverse_minus_identity(mms, level_masks, base_mask)
        rhss = [jnp.concatenate([v_s[pl.ds(r0, gw), h * HD:(h + 1) * HD] * beta_c, kb * jnp.exp(gcc)], axis=1)
                for (h, r0, kb, _, _, gcc, beta_c, _) in chains]
        uws = [rhs + _dot(y.astype(BF16), rhs.astype(BF16)) for rhs, y in zip(rhss, ys)]
        for (h, r0, _, _, dec, _, _, qk), uw in zip(chains, uws):
            u_s[h, pl.ds(r0, gw), :] = uw[:, :HD]
            w_s[h, pl.ds(r0, gw), :] = uw[:, HD:].astype(BF16)
            at_s[h, pl.ds(r0, gw), :] = (qk * dec).astype(BF16)
        return carry

    lax.fori_loop(0, ng // ilv, solve_groups, 0)

    nw = nw_ref[...]

    def step(c, states):
        r0 = pl.multiple_of(c * C, C)
        kcs = [k_s[pl.ds(r0, C), qh * HD:(qh + 1) * HD] for qh in range(nh)]
        qcs = [q_s[pl.ds(r0, C), qh * HD:(qh + 1) * HD] for qh in range(nh)]
        gccs = [gcc_s[h, pl.ds(r0, C), :] for h in heads]
        lhss = [jnp.concatenate([w_s[h, pl.ds(r0, C), :], (qcs[h // 2] * jnp.exp(gccs[h])).astype(BF16)], axis=0)
                for h in heads]
        wss = [_dot(lhss[h], states[h].astype(BF16)) for h in heads]
        vns = [(u_s[h, pl.ds(r0, C), :] - wss[h][:C]).astype(BF16) for h in heads]
        glast = [gccs[h][C - 1:C, :] for h in heads]
        kes = [(kcs[h // 2] * jnp.exp(glast[h] - gccs[h])).astype(BF16) for h in heads]
        outs = [wss[h][C:] + _dot(at_s[h, pl.ds(r0, C), :], vns[h]) for h in heads]
        new_states = [states[h] * jnp.exp(glast[h]) + _dot_tn(kes[h], vns[h]) for h in heads]
        for h in heads:
            zc = z_ref[pl.ds(r0, C), h * HD:(h + 1) * HD].astype(F32)
            o_ref[pl.ds(r0, C), h * HD:(h + 1) * HD] = (_rms_rows(outs[h]) * nw * _silu(zc)).astype(BF16)
        return tuple(new_states)

    s_fin = lax.fori_loop(0, nc, step, tuple(s0_ref[h] for h in heads))
    for h in heads:
        sout_ref[h] = s_fin[h]


def _gdn_core(p, prev8, conv_w_all, b_rows, a_rows, alog_b, dtb_b, norm_w_all, s0, layer,
              batch, t_len, t_valid):
    HD = GDN_HEAD
    ng8, gw = b_rows.shape[2], b_rows.shape[3]
    C = gw
    ng = t_len // gw
    alog_b = jnp.broadcast_to(alog_b[:, :, None, None], alog_b.shape + (1, gw))
    dtb_b = jnp.broadcast_to(dtb_b[:, :, None, None], dtb_b.shape + (1, gw))
    step_bytes = (12 * p.dtype.itemsize + 56) * t_len * HD
    nh = GDN_MAX_HEADS_PER_STEP
    while nh > 1 and step_bytes * nh > GDN_STEP_VMEM_BYTES:
        nh //= 2
    ilv = math.gcd(ng, max(1, GDN_SOLVE_INTERLEAVE // nh))
    qw, vw, nv = nh * HD, 2 * nh * HD, 2 * nh
    kq = GDN_KEY_DIM // qw
    kz = GDN_CONV_DIM // vw
    vmem = step_bytes * nh + 16 * 2**20
    grid = (batch, GDN_QK_HEADS // nh)
    return pl.pallas_call(
        functools.partial(_gdn_body, t_len=t_len, t_valid=t_valid, gw=gw, ilv=ilv),
        grid=grid,
        in_specs=[
            pl.BlockSpec((t_len, qw), lambda b, h: (b, h)),
            pl.BlockSpec((t_len, qw), lambda b, h: (b, kq + h)),
            pl.BlockSpec((t_len, vw), lambda b, h: (b, kq + h)),
            pl.BlockSpec((t_len, vw), lambda b, h: (b, kz + h)),
            pl.BlockSpec((None, 8, qw), lambda b, h: (b, 0, h)),
            pl.BlockSpec((None, 8, qw), lambda b, h: (b, 0, kq + h)),
            pl.BlockSpec((None, 8, vw), lambda b, h: (b, 0, kq + h)),
            pl.BlockSpec((None, GDN_CONV, qw), lambda b, h: (layer, 0, h)),
            pl.BlockSpec((None, GDN_CONV, qw), lambda b, h: (layer, 0, kq + h)),
            pl.BlockSpec((None, GDN_CONV, vw), lambda b, h: (layer, 0, kq + h)),
            pl.BlockSpec((None, nv, ng8, gw), lambda b, h: (b, h, 0, 0)),
            pl.BlockSpec((None, nv, ng8, gw), lambda b, h: (b, h, 0, 0)),
            pl.BlockSpec((None, nv, 1, gw), lambda b, h: (layer, h, 0, 0)),
            pl.BlockSpec((None, nv, 1, gw), lambda b, h: (layer, h, 0, 0)),
            pl.BlockSpec((None, 1, HD), lambda b, h: (layer, 0, 0)),
            pl.BlockSpec((None, nv, HD, HD), lambda b, h: (b, h, 0, 0)),
        ],
        out_specs=[
            pl.BlockSpec((t_len, vw), lambda b, h: (b, h)),
            pl.BlockSpec((None, nv, HD, HD), lambda b, h: (b, h, 0, 0)),
        ],
        out_shape=[
            jax.ShapeDtypeStruct((batch * t_len, GDN_VAL_DIM), BF16),
            jax.ShapeDtypeStruct((batch, GDN_V_HEADS, HD, HD), F32),
        ],
        scratch_shapes=[
            pltpu.VMEM((t_len, qw), F32),
            pltpu.VMEM((t_len, qw), F32),
            pltpu.VMEM((t_len, vw), F32),
            pltpu.VMEM((nv, t_len, HD), F32),
            pltpu.VMEM((nv, t_len, HD), BF16),
            pltpu.VMEM((nv, t_len, C), BF16),
            pltpu.VMEM((nv, ng8, gw), F32),
            pltpu.VMEM((nv, t_len, HD), F32),
            pltpu.VMEM((nv, t_len, HD), F32),
        ],
        compiler_params=_cparams(("parallel", "arbitrary"), vmem),
        name="gdn_core",
    )(p, p, p, p, prev8, prev8, prev8, conv_w_all, conv_w_all, conv_w_all,
      b_rows, a_rows, alog_b, dtb_b, norm_w_all, s0)


def _diff_lambda(lq1_ref, lk1_ref, lq2_ref, lk2_ref, lam_init):
    s1 = jnp.sum(lq1_ref[...] * lk1_ref[...], axis=-1, keepdims=True)
    s2 = jnp.sum(lq2_ref[...] * lk2_ref[...], axis=-1, keepdims=True)
    return jnp.exp(s1) - jnp.exp(s2) + lam_init


def _flash_body(q_ref, k_ref, v_ref, lq1_ref, lk1_ref, lq2_ref, lk2_ref, sn_ref, o_ref,
                m_s, l_s, acc_s, *, tq, sub, lam_init):
    qi = pl.program_id(2)
    scale = DIFF_HEAD ** -0.5
    q = q_ref[...]
    q_maps = (q[:, :DIFF_HEAD].astype(BF16), q[:, DIFF_HEAD:].astype(BF16))
    m_s[...] = jnp.full(m_s.shape, -jnp.inf, F32)
    l_s[...] = jnp.zeros(l_s.shape, F32)
    acc_s[...] = jnp.zeros(acc_s.shape, F32)
    c2 = scale * math.log2(math.e)
    maps = (0, 1)

    def block(kb, masked):
        r0 = pl.multiple_of(kb * tq, tq)
        k = k_ref[pl.ds(r0, tq), :]
        v = v_ref[pl.ds(r0, tq), :].astype(BF16)
        ks = [k[:, c * DIFF_HEAD:(c + 1) * DIFF_HEAD].astype(BF16) for c in maps]
        chains = [(r, c) for r in range(tq // sub) for c in maps]

        def scores(r, c):
            return _dot_nt(q_maps[c][r * sub:(r + 1) * sub, :], ks[c])

        s_next = scores(*chains[0])
        for i, (r, c) in enumerate(chains):
            s = s_next
            if i + 1 < len(chains):
                s_next = scores(*chains[i + 1])
            rs = slice(r * sub, (r + 1) * sub)
            if masked:
                keep = (lax.broadcasted_iota(jnp.int32, (sub, tq), 1)
                        <= lax.broadcasted_iota(jnp.int32, (sub, tq), 0) + r * sub)
                s = jnp.where(keep, s, -jnp.inf)
            m_prev = m_s[c, rs, :]
            m_new = jnp.maximum(m_prev, jnp.max(s, axis=-1, keepdims=True))
            p = jnp.exp2((s - jnp.concatenate([m_new] * (tq // V7X_LANES), axis=1)) * c2)
            alpha = jnp.exp2((m_prev - m_new) * c2)
            p_lanes = p[:, :V7X_LANES]
            for t in range(1, tq // V7X_LANES):
                p_lanes = p_lanes + p[:, t * V7X_LANES:(t + 1) * V7X_LANES]
            l_s[c, rs, :] = alpha * l_s[c, rs, :] + p_lanes
            acc_s[c, rs, :] = (jnp.concatenate([alpha] * (DIFF_V_HEAD // V7X_LANES), axis=1) * acc_s[c, rs, :]
                               + _dot(p.astype(BF16), v))
            m_s[c, rs, :] = m_new

    def body(kb, carry):
        block(kb, False)
        return carry

    lax.fori_loop(0, qi, body, 0)
    block(qi, True)
    lam = _diff_lambda(lq1_ref, lk1_ref, lq2_ref, lk2_ref, lam_init)
    l0 = jnp.sum(l_s[0], axis=-1, keepdims=True)
    l1 = jnp.sum(l_s[1], axis=-1, keepdims=True)
    o = acc_s[0] / l0 - lam * (acc_s[1] / l1)
    o_ref[...] = (_rms_rows(o) * sn_ref[...] * (1.0 - lam_init)).astype(BF16)


def _flash(qk, v, lq1, lk1, lq2, lk2, sn_all, w_layer, lam_init, batch, t_len, tq):
    nq = t_len // tq
    dv = DIFF_V_HEAD
    vmem = 2 * tq * dv * 4 + 4 * t_len * dv * 4 + 2 * tq * dv * 2 + 2 * tq * dv * 4 + 8 * tq * tq * 4 + 4 * 2**20
    lam_spec = pl.BlockSpec((None, 1, DIFF_HEAD), lambda b, h, i: (w_layer, 0, 0))
    return pl.pallas_call(
        functools.partial(_flash_body, tq=tq, sub=min(FLASH_SUB_ROWS, tq), lam_init=lam_init),
        grid=(batch, DIFF_HEADS, nq),
        in_specs=[
            pl.BlockSpec((None, tq, dv), lambda b, h, i: (0, b * nq + i, h)),
            pl.BlockSpec((None, t_len, dv), lambda b, h, i: (1, b, h)),
            pl.BlockSpec((t_len, dv), lambda b, h, i: (b, h)),
            lam_spec, lam_spec, lam_spec, lam_spec,
            pl.BlockSpec((None, 1, dv), lambda b, h, i: (w_layer, 0, 0)),
        ],
        out_specs=pl.BlockSpec((tq, dv), lambda b, h, i: (b * nq + i, h)),
        out_shape=jax.ShapeDtypeStruct((batch * t_len, DIFF_V_DIM), BF16),
        scratch_shapes=[
            pltpu.VMEM((2, tq, V7X_LANES), F32),
            pltpu.VMEM((2, tq, V7X_LANES), F32),
            pltpu.VMEM((2, tq, dv), F32),
        ],
        compiler_params=_cparams(("parallel", "parallel", "arbitrary"), vmem),
        name="diff_flash",
    )(qk, qk, v, lq1, lk1, lq2, lk2, sn_all)


def _decode_body(pt_ref, q_ref, kn_ref, vn_ref, *rest, n_steps, pps, n_new, lam_init):
    del pt_ref
    kc_refs, vc_refs = rest[:pps], rest[pps:2 * pps]
    lq1_ref, lk1_ref, lq2_ref, lk2_ref, sn_ref, o_ref, m_s, l_s, acc_s = rest[2 * pps:]
    p = pl.program_id(1)
    scale = DIFF_HEAD ** -0.5
    n_half = q_ref.shape[1]
    n_rows = 2 * n_half
    n_cols = vn_ref.shape[0]

    @pl.when(p == 0)
    def _():
        m_s[...] = jnp.full(m_s.shape, -jnp.inf, F32)
        l_s[...] = jnp.zeros(l_s.shape, F32)
        acc_s[...] = jnp.zeros(acc_s.shape, F32)

    row_head = (lax.broadcasted_iota(jnp.int32, (n_rows, n_cols), 0) % n_half) // n_new
    col = lax.broadcasted_iota(jnp.int32, (n_rows, n_cols), 1)
    head_ok = row_head == col % DIFF_HEADS

    q_bf = [q_ref[c].astype(BF16) for c in range(2)]

    def update(k_refs, v_refs, mask):
        ss = [jnp.concatenate(
            [_dot_nt(q_bf[c], k_ref[pl.ds(c, n_cols, stride=2), :].astype(BF16)) for c in range(2)],
            axis=0) * scale for k_ref in k_refs]
        ss = [jnp.where(mask, s, -jnp.inf) for s in ss]
        m_prev = m_s[...]
        m_new = m_prev
        for s in ss:
            m_new = jnp.maximum(m_new, jnp.max(s, axis=-1, keepdims=True))
        alpha = jnp.exp(m_prev - m_new)
        prs = [jnp.exp(s - m_new) for s in ss]
        l_new = alpha * l_s[...]
        for pr in prs:
            l_new = l_new + jnp.sum(pr, axis=-1, keepdims=True)
        pvs = [_dot(pr.astype(BF16), v_ref[...].astype(BF16)) for pr, v_ref in zip(prs, v_refs)]
        acc = alpha * acc_s[...]
        for pv in pvs:
            acc = acc + pv
        l_s[...] = l_new
        m_s[...] = m_new
        acc_s[...] = acc

    update(kc_refs, vc_refs, head_ok)

    @pl.when(p == n_steps - 1)
    def _():
        qidx = lax.broadcasted_iota(jnp.int32, (n_rows, n_cols), 0) % n_new
        causal = col // DIFF_HEADS <= qidx
        update([kn_ref], [vn_ref], jnp.logical_and(head_ok, causal))
        lam = _diff_lambda(lq1_ref, lk1_ref, lq2_ref, lk2_ref, lam_init)
        o = acc_s[...] / l_s[...]
        od = o[:n_half] - lam * o[n_half:]
        o_ref[...] = _rms_rows(od) * sn_ref[...] * (1.0 - lam_init)


def _decode(page_table, q_maps, kn_pad, vn_pad, cache_k, cache_v, lq1, lk1, lq2, lk2, sn_all,
            w_layer, lam_init, n_new):
    bs, n_pages = page_table.shape
    n_half = q_maps.shape[2]
    k_rows, v_rows = cache_k.shape[2], cache_v.shape[2]
    pps = math.gcd(n_pages, DECODE_PAGES_PER_STEP)
    n_steps = n_pages // pps
    lam_spec = pl.BlockSpec((None, 1, DIFF_HEAD), lambda b, p, pt: (w_layer, 0, 0))
    vmem = 3 * pps * (k_rows * DIFF_HEAD + v_rows * DIFF_V_HEAD) * 4 + 20 * 2**20

    def page_spec(rows, width, i):
        return pl.BlockSpec((None, None, rows, width), lambda b, p, pt: (w_layer, pt[b, p * pps + i], 0, 0))

    grid_spec = pltpu.PrefetchScalarGridSpec(
        num_scalar_prefetch=1,
        grid=(bs, n_steps),
        in_specs=[
            pl.BlockSpec((None, 2, n_half, DIFF_HEAD), lambda b, p, pt: (b, 0, 0, 0)),
            pl.BlockSpec((None, k_rows, DIFF_HEAD), lambda b, p, pt: (b, 0, 0)),
            pl.BlockSpec((None, v_rows, DIFF_V_HEAD), lambda b, p, pt: (b, 0, 0)),
            *[page_spec(k_rows, DIFF_HEAD, i) for i in range(pps)],
            *[page_spec(v_rows, DIFF_V_HEAD, i) for i in range(pps)],
            lam_spec, lam_spec, lam_spec, lam_spec,
            pl.BlockSpec((None, 1, DIFF_V_HEAD), lambda b, p, pt: (w_layer, 0, 0)),
        ],
        out_specs=pl.BlockSpec((None, n_half, DIFF_V_HEAD), lambda b, p, pt: (b, 0, 0)),
        scratch_shapes=[
            pltpu.VMEM((2 * n_half, 1), F32),
            pltpu.VMEM((2 * n_half, 1), F32),
            pltpu.VMEM((2 * n_half, DIFF_V_HEAD), F32),
        ],
    )
    return pl.pallas_call(
        functools.partial(_decode_body, n_steps=n_steps, pps=pps, n_new=n_new, lam_init=lam_init),
        grid_spec=grid_spec,
        out_shape=jax.ShapeDtypeStruct((bs, n_half, DIFF_V_HEAD), F32),
        compiler_params=_cparams(("parallel", "arbitrary"), vmem),
        name="diff_decode",
    )(page_table, q_maps, kn_pad, vn_pad, *([cache_k] * pps), *([cache_v] * pps), lq1, lk1, lq2, lk2, sn_all)


def _rope_tables(pos):
    half = ROPE_DIM // 2
    inv = ROPE_THETA ** (-jnp.arange(half, dtype=F32) * 2.0 / ROPE_DIM)
    ang = pos.astype(F32)[:, None] * inv[None, :]
    cos, sin = jnp.cos(ang), jnp.sin(ang)
    n = pos.shape[0]
    tail = DIFF_HEAD - ROPE_DIM
    c = jnp.concatenate([cos, cos, jnp.ones((n, tail), F32)], axis=1)
    sa = jnp.concatenate([-sin, jnp.zeros((n, half + tail), F32)], axis=1)
    sb = jnp.concatenate([jnp.zeros((n, half), F32), sin, jnp.zeros((n, tail), F32)], axis=1)
    return c, sa, sb


def _gate_rows(ba, batch, t_len):
    gw = min(GDN_SOLVE_WIDTH, t_len)
    ng = t_len // gw
    ba = ba[:, :2 * GDN_V_HEADS]
    ba = ba.reshape(batch, ng, gw, 2, GDN_V_HEADS).transpose(3, 0, 4, 1, 2)
    ng8 = -(-ng // 8) * 8
    ba = jnp.pad(ba, ((0, 0), (0, 0), (0, 0), (0, ng8 - ng), (0, 0)))
    return ba[0], ba[1]


def kernel(x_prompt, x_sample, state_delta, state_conv, cache_k, cache_v, page_table, norm_ffn1, ffn1_w_gu, ffn1_w_dn, norm_mix, norm_ffn2, ffn2_w_gu, ffn2_w_dn, gdn_w_in, gdn_conv_w, gdn_a_log, gdn_dt_bias, gdn_norm_w, gdn_w_out, diff_w_in, diff_q_norm, diff_k_norm, diff_lam_q1, diff_lam_k1, diff_lam_q2, diff_lam_k2, diff_sub_norm, diff_w_out):
    bp, tp, d = x_prompt.shape
    bs, ts, _ = x_sample.shape
    depth = norm_ffn1.shape[0]
    n_pages = page_table.shape[1]
    past = n_pages * PAGE_SIZE
    C = GDN_CHUNK

    row = lambda w: w.reshape(w.shape[0], 1, w.shape[1])
    w_gu1, w_dn1, w_gu2, w_dn2 = ffn1_w_gu, ffn1_w_dn, ffn2_w_gu, ffn2_w_dn
    tn = 512
    tn_gdn = 1024
    gdn_n = GDN_CONV_DIM + GDN_VAL_DIM
    n_ba = 2 * GDN_V_HEADS
    w_gin_t = jnp.swapaxes(gdn_w_in, 1, 2)
    w_gout, w_din, w_dout = gdn_w_out, diff_w_in, diff_w_out
    g_ffn1, g_mix, g_ffn2 = row(norm_ffn1), row(norm_mix), row(norm_ffn2)
    alog_b, dtb_b = gdn_a_log, gdn_dt_bias
    gdn_nw = row(gdn_norm_w)
    qk_n = jnp.stack([diff_q_norm, diff_k_norm], axis=1).reshape(-1, 1, DIFF_HEAD)
    lq1, lk1, lq2, lk2 = row(diff_lam_q1), row(diff_lam_k1), row(diff_lam_q2), row(diff_lam_k2)
    sub_n = row(diff_sub_norm)
    ck = cache_k.reshape(cache_k.shape[0], cache_k.shape[1], PAGE_SIZE * DIFF_HEADS * 2, DIFF_HEAD)
    cv = cache_v.reshape(cache_v.shape[0], cache_v.shape[1], PAGE_SIZE * DIFF_HEADS, DIFF_V_HEAD)

    rope_p = _rope_tables(jnp.arange(tp))
    rope_p = tuple(jnp.tile(t, (bp, 1)) for t in rope_p)
    rope_s = _rope_tables(past + jnp.arange(ts))
    rope_s = tuple(jnp.tile(t, (bs, 1)) for t in rope_s)

    xp = x_prompt.reshape(bp * tp, d)
    xs = x_sample.reshape(bs * ts, d)
    mp, ms = bp * tp, bs * ts
    tm_p, tm_big, tm_s = 1024, 1024, ms
    tf, tf_cast = 512, 256
    ts_pad = -(-ts // C) * C

    prev8_p = jnp.zeros((bp, 8, GDN_CONV_DIM), F32)
    s0_p = jnp.zeros((bp, GDN_V_HEADS, GDN_HEAD, GDN_HEAD), F32)

    p_delta, p_conv, p_k, p_v = [], [], [], []
    s_delta, s_conv, s_k, s_v = [], [], [], []
    for i in range(depth):
        xs, wg, wu, wd = _ffn_cast(xs, g_ffn1, w_gu1, w_dn1, i, tf_cast)
        xp = _ffn(xp, g_ffn1, wg, wu, wd, i, tm_p, tf)
        j = i // 2
        if i % 2 == 0:
            pp = _proj(xp, g_mix, w_gin_t, i, j, tm_big, tn_gdn, n_out=gdn_n, transposed=True, out_dtype=BF16)
            ps = _proj(xs, g_mix, w_gin_t, i, j, tm_s, tn_gdn, n_out=gdn_n, transposed=True)
            bap = _proj(xp, g_mix, w_gin_t, i, j, tm_big, n_ba, col0=gdn_n, n_out=n_ba, transposed=True)
            bas = _proj(xs, g_mix, w_gin_t, i, j, tm_s, n_ba, col0=gdn_n, n_out=n_ba, transposed=True)
            b_rows, a_rows = _gate_rows(bap, bp, tp)
            op, sp = _gdn_core(pp, prev8_p, gdn_conv_w, b_rows, a_rows, alog_b, dtb_b, gdn_nw, s0_p, j,
                               bp, tp, tp)
            p_conv.append(pp.reshape(bp, tp, gdn_n)[:, tp - (GDN_CONV - 1):, :GDN_CONV_DIM].astype(F32))
            p_delta.append(sp)
            xp = _outproj(op, w_gout, xp, j, tm_big, tn)

            ps3 = ps.reshape(bs, ts, gdn_n)
            ps_pad = jnp.pad(ps3, ((0, 0), (0, ts_pad - ts), (0, 0))).reshape(bs * ts_pad, gdn_n)
            prev8_s = jnp.pad(state_conv[j], ((0, 0), (8 - (GDN_CONV - 1), 0), (0, 0)))
            bas_pad = jnp.pad(bas.reshape(bs, ts, n_ba), ((0, 0), (0, ts_pad - ts), (0, 0)))
            b_rows, a_rows = _gate_rows(bas_pad.reshape(bs * ts_pad, n_ba), bs, ts_pad)
            os_, ss = _gdn_core(ps_pad, prev8_s, gdn_conv_w, b_rows, a_rows, alog_b, dtb_b, gdn_nw,
                                state_delta[j], j, bs, ts_pad, ts)
            xp_conv = jnp.concatenate([state_conv[j], ps3[:, :, :GDN_CONV_DIM]], axis=1)
            s_conv.append(xp_conv[:, ts:])
            s_delta.append(ss)
            os_ = os_.reshape(bs, ts_pad, GDN_VAL_DIM)[:, :ts].reshape(ms, GDN_VAL_DIM)
            xs = _outproj(os_, w_gout, xs, j, tm_s, tn)
        else:
            lam_init = 0.8 - 0.6 * math.exp(-0.3 * i)
            qkp = _qkproj(xp, g_mix, w_din, qk_n, *rope_p, i, j, tm_big, tn)
            vp = _proj(xp, g_mix, w_din, i, j, tm_big, tn, col0=2 * DIFF_QK_DIM, n_out=DIFF_V_DIM)
            qks = _qkproj(xs, g_mix, w_din, qk_n, *rope_s, i, j, tm_s, tn)
            vs = _proj(xs, g_mix, w_din, i, j, tm_s, tn, col0=2 * DIFF_QK_DIM, n_out=DIFF_V_DIM)
            qs, ks = qks[0], qks[1]
            op = _flash(qkp, vp, lq1, lk1, lq2, lk2, sub_n, j, lam_init, bp, tp, 512)
            xp = _outproj(op, w_dout, xp, j, tm_big, tn)
            p_k.append(qkp[1].reshape(bp, tp, DIFF_HEADS, 2, DIFF_HEAD))
            p_v.append(vp.reshape(bp, tp, DIFF_HEADS, DIFF_V_HEAD))

            q_maps = qs.reshape(bs, ts, DIFF_HEADS, 2, DIFF_HEAD).transpose(0, 3, 2, 1, 4)
            q_maps = q_maps.reshape(bs, 2, DIFF_HEADS * ts, DIFF_HEAD)
            kn_pad = jnp.pad(ks.reshape(bs, ts * DIFF_HEADS * 2, DIFF_HEAD),
                             ((0, 0), (0, (PAGE_SIZE - ts) * DIFF_HEADS * 2), (0, 0)))
            vn_pad = jnp.pad(vs.reshape(bs, ts * DIFF_HEADS, DIFF_V_HEAD),
                             ((0, 0), (0, (PAGE_SIZE - ts) * DIFF_HEADS), (0, 0)))
            od = _decode(page_table, q_maps, kn_pad, vn_pad, ck, cv, lq1, lk1, lq2, lk2, sub_n,
                         j, lam_init, ts)
            od = od.reshape(bs, DIFF_HEADS, ts, DIFF_V_HEAD).transpose(0, 2, 1, 3).reshape(ms, DIFF_V_DIM)
            xs = _outproj(od.astype(BF16), w_dout, xs, j, tm_s, tn)
            s_k.append(ks.reshape(bs, ts, DIFF_HEADS, 2, DIFF_HEAD))
            s_v.append(vs.reshape(bs, ts, DIFF_HEADS, DIFF_V_HEAD))
        xs, wg, wu, wd = _ffn_cast(xs, g_ffn2, w_gu2, w_dn2, i, tf_cast)
        xp = _ffn(xp, g_ffn2, wg, wu, wd, i, tm_p, tf)

    return (xp.reshape(bp, tp, d), xs.reshape(bs, ts, d),
            jnp.stack(p_delta), jnp.stack(p_conv), jnp.stack(p_k), jnp.stack(p_v),
            jnp.stack(s_delta), jnp.stack(s_conv), jnp.stack(s_k), jnp.stack(s_v))
```

```python
import functools
import math

import jax
import jax.numpy as jnp
from jax import lax
from jax.experimental import pallas as pl
from jax.experimental.pallas import tpu as pltpu

F32 = jnp.float32
BF16 = jnp.bfloat16
EPS = 1e-6

D_MODEL = 2048
D_FF = 5632
GDN_QK_HEADS = 16
GDN_V_HEADS = 32
GDN_HEAD = 128
GDN_CONV = 4
GDN_CHUNK = 64
GDN_KEY_DIM = GDN_QK_HEADS * GDN_HEAD
GDN_VAL_DIM = GDN_V_HEADS * GDN_HEAD
GDN_CONV_DIM = 2 * GDN_KEY_DIM + GDN_VAL_DIM
GDN_IN_DIM = GDN_CONV_DIM + GDN_VAL_DIM + 2 * GDN_V_HEADS
DIFF_HEADS = 8
DIFF_HEAD = 128
DIFF_V_HEAD = 2 * DIFF_HEAD
DIFF_QK_DIM = DIFF_HEADS * 2 * DIFF_HEAD
DIFF_V_DIM = DIFF_HEADS * DIFF_V_HEAD
ROPE_DIM = DIFF_HEAD // 4
ROPE_THETA = 500000.0
PAGE_SIZE = 128

V7X_LANES = 128
V7X_VMEM_LIMIT_BYTES = 56 * 1024 * 1024

GDN_SOLVE_WIDTH = 128
GDN_SOLVE_INTERLEAVE = 4
GDN_MAX_HEADS_PER_STEP = 4
GDN_CONV_ROWS = 64
GDN_CONV_TILES_PER_TRIP = 32
GDN_STEP_VMEM_BYTES = 44 * 1024 * 1024
DECODE_PAGES_PER_STEP = 4
FLASH_SUB_ROWS = 256


def _cparams(semantics, vmem_bytes):
    return pltpu.CompilerParams(dimension_semantics=semantics,
                                vmem_limit_bytes=int(min(max(vmem_bytes, 16 * 2**20), V7X_VMEM_LIMIT_BYTES)))


def _dot(a, b):
    return jnp.dot(a, b, preferred_element_type=F32)


def _dot_nt(a, b):
    return lax.dot_general(a, b, (((1,), (1,)), ((), ())), preferred_element_type=F32)


def _dot_tn(a, b):
    return lax.dot_general(a, b, (((0,), (0,)), ((), ())), preferred_element_type=F32)


def _rms_rows(x):
    return x * lax.rsqrt(jnp.mean(x * x, axis=-1, keepdims=True) + EPS)


def _silu(x):
    return x * jax.nn.sigmoid(x)


def _lane_sumsq(x):
    ones = jnp.ones((x.shape[1], V7X_LANES), BF16)
    return _dot((x * x).astype(BF16), ones)


def _ffn_prologue(x_ref, g_ref, o_ref, xn_ref):
    @pl.when(pl.program_id(1) == 0)
    def _():
        x = x_ref[...]
        xn_ref[...] = (_rms_rows(x) * g_ref[...]).astype(BF16)
        o_ref[...] = x


def _ffn_body(x_ref, g_ref, wg_ref, wu_ref, wd_ref, o_ref, xn_ref):
    _ffn_prologue(x_ref, g_ref, o_ref, xn_ref)
    xn = xn_ref[...]
    gate = _dot(xn, wg_ref[...])
    up = _dot(xn, wu_ref[...])
    act = (_silu(gate) * up).astype(BF16)
    o_ref[...] += 0.5 * _dot(act, wd_ref[...])


def _ffn_cast_body(x_ref, g_ref, wg_ref, wu_ref, wd_ref, o_ref, wgb_ref, wub_ref, wdb_ref, xn_ref):
    _ffn_prologue(x_ref, g_ref, o_ref, xn_ref)
    wgb_ref[...] = wg_ref[...].astype(BF16)
    wub_ref[...] = wu_ref[...].astype(BF16)
    wdb_ref[...] = wd_ref[...].astype(BF16)
    xn = xn_ref[...]
    gate = _dot(xn, wgb_ref[...])
    up = _dot(xn, wub_ref[...])
    act = (_silu(gate) * up).astype(BF16)
    o_ref[...] += 0.5 * _dot(act, wdb_ref[...])


def _ffn_cast(x, g_all, wgu_all, wdn_all, layer, tf):
    m, d = x.shape
    f = wdn_all.shape[1]
    nf = f // tf
    vmem = 4 * m * d * 4 + 2 * 3 * d * tf * (4 + 2) + 6 * m * tf * 4 + 3 * d * tf * (4 + 2)
    return pl.pallas_call(
        _ffn_cast_body,
        grid=(1, nf),
        in_specs=[
            pl.BlockSpec((m, d), lambda i, j: (0, 0)),
            pl.BlockSpec((None, 1, d), lambda i, j: (layer, 0, 0)),
            pl.BlockSpec((None, d, tf), lambda i, j: (layer, 0, j)),
            pl.BlockSpec((None, d, tf), lambda i, j: (layer, 0, nf + j)),
            pl.BlockSpec((None, tf, d), lambda i, j: (layer, j, 0)),
        ],
        out_specs=[
            pl.BlockSpec((m, d), lambda i, j: (0, 0)),
            pl.BlockSpec((d, tf), lambda i, j: (0, j)),
            pl.BlockSpec((d, tf), lambda i, j: (0, j)),
            pl.BlockSpec((tf, d), lambda i, j: (j, 0)),
        ],
        out_shape=[
            jax.ShapeDtypeStruct((m, d), F32),
            jax.ShapeDtypeStruct((d, f), BF16),
            jax.ShapeDtypeStruct((d, f), BF16),
            jax.ShapeDtypeStruct((f, d), BF16),
        ],
        scratch_shapes=[pltpu.VMEM((m, d), BF16)],
        compiler_params=_cparams(("arbitrary", "arbitrary"), vmem),
        name="half_ffn_cast",
    )(x, g_all, wgu_all, wgu_all, wdn_all)


def _ffn(x, g_all, wg, wu, wd, layer, tm, tf):
    m, d = x.shape
    f = wd.shape[0]
    vmem = 4 * tm * d * 4 + tm * d * 2 + 2 * 3 * d * tf * 2 + 6 * tm * tf * 4
    return pl.pallas_call(
        _ffn_body,
        grid=(m // tm, f // tf),
        in_specs=[
            pl.BlockSpec((tm, d), lambda i, j: (i, 0)),
            pl.BlockSpec((None, 1, d), lambda i, j: (layer, 0, 0)),
            pl.BlockSpec((d, tf), lambda i, j: (0, j)),
            pl.BlockSpec((d, tf), lambda i, j: (0, j)),
            pl.BlockSpec((tf, d), lambda i, j: (j, 0)),
        ],
        out_specs=pl.BlockSpec((tm, d), lambda i, j: (i, 0)),
        out_shape=jax.ShapeDtypeStruct((m, d), F32),
        scratch_shapes=[pltpu.VMEM((tm, d), BF16)],
        compiler_params=_cparams(("parallel", "arbitrary"), vmem),
        name="half_ffn",
    )(x, g_all, wg, wu, wd)


def _proj_body(x_ref, g_ref, w_ref, *rest, transposed):
    o_ref, xn_ref = rest[-2:]

    @pl.when(pl.program_id(1) == 0)
    def _():
        xn_ref[...] = (_rms_rows(x_ref[...]) * g_ref[...]).astype(BF16)

    w = w_ref[...].astype(BF16)
    acc = _dot_nt(xn_ref[...], w) if transposed else _dot(xn_ref[...], w)
    o_ref[...] = acc.astype(o_ref.dtype)


def _proj(x, g_all, w_all, layer, w_layer, tm, tn, col0=0, n_out=None, transposed=False, out_dtype=F32,
          planes=None, prev=None):
    m, d = x.shape
    n_all = w_all.shape[1] if transposed else w_all.shape[2]
    n_out = n_all - col0 if n_out is None else n_out
    jb = col0 // tn
    vmem = 4 * tm * d * 4 + tm * d * 2 + 2 * d * tn * w_all.dtype.itemsize + 4 * tm * tn * 4 + d * tn * 2
    if transposed:
        w_spec = pl.BlockSpec((None, tn, d), lambda i, j: (w_layer, jb + j, 0))
    else:
        w_spec = pl.BlockSpec((None, d, tn), lambda i, j: (w_layer, 0, jb + j))
    in_specs = [
        pl.BlockSpec((tm, d), lambda i, j: (i, 0)),
        pl.BlockSpec((None, 1, d), lambda i, j: (layer, 0, 0)),
        w_spec,
    ]
    args = [x, g_all, w_all]
    aliases = {}
    if planes is None:
        out_spec = pl.BlockSpec((tm, tn), lambda i, j: (i, j))
        out_shape = jax.ShapeDtypeStruct((m, n_out), out_dtype)
    else:
        out_spec = pl.BlockSpec((None, tm, tn), lambda i, j: (w_layer, i, j))
        out_shape = jax.ShapeDtypeStruct((planes, m, n_out), out_dtype)
        if prev is not None:
            in_specs.append(pl.BlockSpec(memory_space=pl.ANY))
            args.append(prev)
            aliases = {len(args) - 1: 0}
    return pl.pallas_call(
        functools.partial(_proj_body, transposed=transposed),
        grid=(m // tm, n_out // tn),
        in_specs=in_specs,
        out_specs=out_spec,
        out_shape=out_shape,
        input_output_aliases=aliases,
        scratch_shapes=[pltpu.VMEM((tm, d), BF16)],
        compiler_params=_cparams(("parallel", "arbitrary"), vmem),
        name="norm_proj",
    )(*args)


def _qkproj_body(x_ref, g_ref, w_ref, nw_ref, cos_ref, sa_ref, sb_ref, *rest):
    o_ref, xn_ref = rest[-2:]

    @pl.when(pl.program_id(1) == 0)
    def _():
        xn_ref[...] = (_rms_rows(x_ref[...]) * g_ref[...]).astype(BF16)

    xn = xn_ref[...]
    wn = nw_ref[...]
    cos, sa, sb = cos_ref[...], sa_ref[...], sb_ref[...]
    pair = 2 * DIFF_HEAD
    n_pairs = o_ref.shape[1] // pair

    def mm(g):
        return _dot(xn, w_ref[:, g * pair:(g + 1) * pair].astype(BF16))

    acc_next = mm(0)
    for g in range(n_pairs):
        acc = acc_next
        if g + 1 < n_pairs:
            acc_next = mm(g + 1)
        for h in range(2):
            y = acc[:, h * DIFF_HEAD:(h + 1) * DIFF_HEAD]
            y = y * lax.rsqrt(_lane_sumsq(y) * (1.0 / DIFF_HEAD) + EPS) * wn
            y = (y * cos + pltpu.roll(y, DIFF_HEAD - ROPE_DIM // 2, 1) * sa
                 + pltpu.roll(y, ROPE_DIM // 2, 1) * sb)
            c0 = g * pair + h * DIFF_HEAD
            o_ref[:, c0:c0 + DIFF_HEAD] = y


def _qkproj(x, g_all, w_all, qk_norm, cos, sa, sb, layer, w_layer, tm, tn, prev=None):
    m, d = x.shape
    nq = DIFF_QK_DIM // tn
    planes = 1 + w_all.shape[0]
    vmem = (4 * tm * d * 4 + tm * d * 2 + 2 * d * tn * w_all.dtype.itemsize + 6 * tm * tn * 4
            + 6 * tm * 128 * 4 + d * tn * 2)
    in_specs = [
        pl.BlockSpec((tm, d), lambda i, j: (i, 0)),
        pl.BlockSpec((None, 1, d), lambda i, j: (layer, 0, 0)),
        pl.BlockSpec((None, d, tn), lambda i, j: (w_layer, 0, j)),
        pl.BlockSpec((None, 1, DIFF_HEAD), lambda i, j: (2 * w_layer + j // nq, 0, 0)),
        pl.BlockSpec((tm, DIFF_HEAD), lambda i, j: (i, 0)),
        pl.BlockSpec((tm, DIFF_HEAD), lambda i, j: (i, 0)),
        pl.BlockSpec((tm, DIFF_HEAD), lambda i, j: (i, 0)),
    ]
    args = [x, g_all, w_all, qk_norm, cos, sa, sb]
    aliases = {}
    if prev is not None:
        in_specs.append(pl.BlockSpec(memory_space=pl.ANY))
        args.append(prev)
        aliases = {len(args) - 1: 0}
    return pl.pallas_call(
        _qkproj_body,
        grid=(m // tm, 2 * nq),
        in_specs=in_specs,
        out_specs=pl.BlockSpec((None, tm, tn), lambda i, j: ((j // nq) * (1 + w_layer), i, j % nq)),
        out_shape=jax.ShapeDtypeStruct((planes, m, DIFF_QK_DIM), F32),
        input_output_aliases=aliases,
        scratch_shapes=[pltpu.VMEM((tm, d), BF16)],
        compiler_params=_cparams(("parallel", "arbitrary"), vmem),
        name="diff_qk_proj",
    )(*args)


def _outproj_body(y_ref, w_ref, x_ref, o_ref):
    o_ref[...] = x_ref[...] + _dot(y_ref[...], w_ref[...].astype(BF16))


def _outproj(y, w_all, x, w_layer, tm, tn):
    m, k = y.shape
    d = x.shape[1]
    vmem = 2 * tm * k * 2 + 2 * k * tn * w_all.dtype.itemsize + 6 * tm * tn * 4 + k * tn * 2
    return pl.pallas_call(
        _outproj_body,
        grid=(m // tm, d // tn),
        in_specs=[
            pl.BlockSpec((tm, k), lambda i, j: (i, 0)),
            pl.BlockSpec((None, k, tn), lambda i, j: (w_layer, 0, j)),
            pl.BlockSpec((tm, tn), lambda i, j: (i, j)),
        ],
        out_specs=pl.BlockSpec((tm, tn), lambda i, j: (i, j)),
        out_shape=jax.ShapeDtypeStruct((m, d), F32),
        compiler_params=_cparams(("parallel", "arbitrary"), vmem),
        name="out_proj",
    )(y, w_all, x)


def _unit_lower_inverse_minus_identity(mms, level_masks, base_mask):
    ys = [-(mm * base_mask) for mm in mms]
    for mask in level_masks:
        os_ = [mm * mask for mm in mms]
        y_bf = [y.astype(BF16) for y in ys]
        zs = [o + _dot(yb, o.astype(BF16)) for o, yb in zip(os_, y_bf)]
        ys = [y - z - _dot(z.astype(BF16), yb) for y, z, yb in zip(ys, zs, y_bf)]
    return ys


def _gdn_body(qp_ref, kp_ref, vp_ref, z_ref, q8_ref, k8_ref, v8_ref, cwq_ref, cwk_ref, cwv_ref,
              b_ref, a_ref, alog_ref, dtb_ref, nw_ref, s0_ref,
              o_ref, sout_ref,
              q_s, k_s, v_s, u_s, w_s, at_s, grow_s, gcc_s, bc_s,
              *, t_len, t_valid, gw, ilv):
    C = gw
    HD = GDN_HEAD
    nc = t_len // C
    ng = t_len // gw
    cpg = gw // C
    nh = q_s.shape[1] // HD
    heads = tuple(range(2 * nh))
    rt = min(GDN_CONV_ROWS, t_len)

    def conv_block(x_ref, p8_ref, cw_ref, r0, cs, first):
        head = p8_ref[:, cs] if first else x_ref[pl.ds(r0 - 16, 16), cs].astype(F32)[8:, :]
        cur = x_ref[pl.ds(r0, rt), cs].astype(F32)
        ext = jnp.concatenate([head, cur], axis=0)
        w = cw_ref[:, cs]
        y = ext[5:5 + rt, :] * w[0:1, :]
        y = y + ext[6:6 + rt, :] * w[1:2, :]
        y = y + ext[7:7 + rt, :] * w[2:3, :]
        y = y + cur * w[3:4, :]
        y = _silu(y)
        if t_valid < t_len:
            rows = r0 + lax.broadcasted_iota(jnp.int32, y.shape, 0)
            y = jnp.where(rows < t_valid, y, 0.0)
        return y

    def l2n(x):
        return x * lax.rsqrt(jnp.sum(x * x, axis=-1, keepdims=True) + EPS)

    def conv_rows(r0, first):
        for qh in range(nh):
            cs = slice(qh * HD, (qh + 1) * HD)
            q_s[pl.ds(r0, rt), cs] = l2n(conv_block(qp_ref, q8_ref, cwq_ref, r0, cs, first)) * (HD ** -0.5)
            k_s[pl.ds(r0, rt), cs] = l2n(conv_block(kp_ref, k8_ref, cwk_ref, r0, cs, first))
        for h in heads:
            cs = slice(h * HD, (h + 1) * HD)
            v_s[pl.ds(r0, rt), cs] = conv_block(vp_ref, v8_ref, cwv_ref, r0, cs, first)

    per_trip = math.gcd(t_len // rt, GDN_CONV_TILES_PER_TRIP)
    conv_rows(0, True)
    for s in range(1, per_trip):
        conv_rows(s * rt, False)

    def conv_body(r, carry):
        for s in range(per_trip):
            conv_rows(pl.multiple_of((r * per_trip + s) * rt, rt), False)
        return carry

    lax.fori_loop(1, t_len // (rt * per_trip), conv_body, 0)

    ri = lax.broadcasted_iota(jnp.int32, (gw, gw), 0)
    ci = lax.broadcasted_iota(jnp.int32, (gw, gw), 1)
    chunk_shift = C.bit_length() - 1
    same_chunk = (ri >> chunk_shift) == (ci >> chunk_shift)
    lower = jnp.logical_and(same_chunk, ci <= ri)
    strict = jnp.logical_and(same_chunk, ci < ri)
    upper_ones = jnp.where(jnp.logical_and(same_chunk, ri <= ci), 1.0, 0.0)
    eye = jnp.where(ri == ci, 1.0, 0.0)
    hi = lax.Precision.HIGHEST
    for h in heads:
        beta = jax.nn.sigmoid(b_ref[h])
        g = -jnp.exp(alog_ref[h]) * jax.nn.softplus(a_ref[h] + dtb_ref[h])
        if t_valid < t_len:
            pos = (lax.broadcasted_iota(jnp.int32, g.shape, 0) * gw
                   + lax.broadcasted_iota(jnp.int32, g.shape, 1))
            g = jnp.where(pos < t_valid, g, 0.0)
            beta = jnp.where(pos < t_valid, beta, 0.0)
        gc = jnp.dot(g, upper_ones, precision=hi, preferred_element_type=F32)
        grow_s[h] = gc
        gc_t = lax.dot_general(eye, gc, (((1,), (1,)), ((), ())), precision=hi,
                               preferred_element_type=F32)
        beta_t = lax.dot_general(eye, beta, (((1,), (1,)), ((), ())), precision=hi,
                                 preferred_element_type=F32)
        for i in range(ng):
            gcc_s[h, i * gw:(i + 1) * gw, :] = jnp.broadcast_to(gc_t[:, i:i + 1], (gw, HD))
            bc_s[h, i * gw:(i + 1) * gw, :] = jnp.broadcast_to(beta_t[:, i:i + 1], (gw, HD))

    base_mask = jnp.where((ri >> 1) == (ci >> 1), 1.0, 0.0)
    level_masks = []
    for s in range(1, chunk_shift):
        same_big = (ri >> (s + 1)) == (ci >> (s + 1))
        same_small = (ri >> s) == (ci >> s)
        level_masks.append(jnp.where(same_big, 1.0, 0.0) - jnp.where(same_small, 1.0, 0.0))

    def lane_tile(x):
        if gw <= HD:
            return x[:, :gw]
        return jnp.concatenate([x] * (gw // HD), axis=1)

    def solve_groups(it, carry):
        chains = []
        for gg in range(ilv):
            gi = it * ilv + gg
            r0 = pl.multiple_of(gi * gw, gw)
            for qh in range(nh):
                kc = k_s[pl.ds(r0, gw), qh * HD:(qh + 1) * HD]
                k_bf = kc.astype(BF16)
                qk = _dot_nt(q_s[pl.ds(r0, gw), qh * HD:(qh + 1) * HD].astype(BF16), k_bf)
                for h in (2 * qh, 2 * qh + 1):
                    gcc = gcc_s[h, pl.ds(r0, gw), :]
                    gd = lane_tile(gcc) - grow_s[h, pl.ds(gi, 1), :]
                    dec = jnp.where(lower, jnp.exp(jnp.where(lower, gd, 0.0)), 0.0)
                    beta_c = bc_s[h, pl.ds(r0, gw), :]
                    chains.append((h, r0, kc * beta_c, k_bf, dec, gcc, beta_c, qk))
        mms = [jnp.where(strict, _dot_nt(kb.astype(BF16), k_bf) * dec, 0.0)
               for (_, _, kb, k_bf, dec, _, _, _) in chains]
        ys = _unit_lower_inverse_minus_identity(mms, level_masks, base_mask)
        rhss = [jnp.concatenate([v_s[pl.ds(r0, gw), h * HD:(h + 1) * HD] * beta_c, kb * jnp.exp(gcc)], axis=1)
                for (h, r0, kb, _, _, gcc, beta_c, _) in chains]
        uws = [rhs + _dot(y.astype(BF16), rhs.astype(BF16)) for rhs, y in zip(rhss, ys)]
        for (h, r0, _, _, dec, _, _, qk), uw in zip(chains, uws):
            u_s[h, pl.ds(r0, gw), :] = uw[:, :HD]
            w_s[h, pl.ds(r0, gw), :] = uw[:, HD:].astype(BF16)
            at_s[h, pl.ds(r0, gw), :] = (qk * dec).astype(BF16)
        return carry

    lax.fori_loop(0, ng // ilv, solve_groups, 0)

    nw = nw_ref[...]

    def step(c, states):
        r0 = pl.multiple_of(c * C, C)
        kcs = [k_s[pl.ds(r0, C), qh * HD:(qh + 1) * HD] for qh in range(nh)]
        qcs = [q_s[pl.ds(r0, C), qh * HD:(qh + 1) * HD] for qh in range(nh)]
        gccs = [gcc_s[h, pl.ds(r0, C), :] for h in heads]
        lhss = [jnp.concatenate([w_s[h, pl.ds(r0, C), :], (qcs[h // 2] * jnp.exp(gccs[h])).astype(BF16)], axis=0)
                for h in heads]
        wss = [_dot(lhss[h], states[h].astype(BF16)) for h in heads]
        vns = [(u_s[h, pl.ds(r0, C), :] - wss[h][:C]).astype(BF16) for h in heads]
        glast = [gccs[h][C - 1:C, :] for h in heads]
        kes = [(kcs[h // 2] * jnp.exp(glast[h] - gccs[h])).astype(BF16) for h in heads]
        outs = [wss[h][C:] + _dot(at_s[h, pl.ds(r0, C), :], vns[h]) for h in heads]
        new_states = [states[h] * jnp.exp(glast[h]) + _dot_tn(kes[h], vns[h]) for h in heads]
        for h in heads:
            zc = z_ref[pl.ds(r0, C), h * HD:(h + 1) * HD].astype(F32)
            o_ref[pl.ds(r0, C), h * HD:(h + 1) * HD] = (_rms_rows(outs[h]) * nw * _silu(zc)).astype(BF16)
        return tuple(new_states)

    s_fin = lax.fori_loop(0, nc, step, tuple(s0_ref[h] for h in heads))
    for h in heads:
        sout_ref[h] = s_fin[h]


def _gdn_core(p, prev8, conv_w_all, b_rows, a_rows, alog_b, dtb_b, norm_w_all, s0, layer,
              batch, t_len, t_valid):
    HD = GDN_HEAD
    ng8, gw = b_rows.shape[2], b_rows.shape[3]
    C = gw
    ng = t_len // gw
    alog_b = jnp.broadcast_to(alog_b[:, :, None, None], alog_b.shape + (1, gw))
    dtb_b = jnp.broadcast_to(dtb_b[:, :, None, None], dtb_b.shape + (1, gw))
    step_bytes = (12 * p.dtype.itemsize + 56) * t_len * HD
    nh = GDN_MAX_HEADS_PER_STEP
    while nh > 1 and step_bytes * nh > GDN_STEP_VMEM_BYTES:
        nh //= 2
    ilv = math.gcd(ng, max(1, GDN_SOLVE_INTERLEAVE // nh))
    qw, vw, nv = nh * HD, 2 * nh * HD, 2 * nh
    kq = GDN_KEY_DIM // qw
    kz = GDN_CONV_DIM // vw
    vmem = step_bytes * nh + 16 * 2**20
    grid = (batch, GDN_QK_HEADS // nh)
    return pl.pallas_call(
        functools.partial(_gdn_body, t_len=t_len, t_valid=t_valid, gw=gw, ilv=ilv),
        grid=grid,
        in_specs=[
            pl.BlockSpec((t_len, qw), lambda b, h: (b, h)),
            pl.BlockSpec((t_len, qw), lambda b, h: (b, kq + h)),
            pl.BlockSpec((t_len, vw), lambda b, h: (b, kq + h)),
            pl.BlockSpec((t_len, vw), lambda b, h: (b, kz + h)),
            pl.BlockSpec((None, 8, qw), lambda b, h: (b, 0, h)),
            pl.BlockSpec((None, 8, qw), lambda b, h: (b, 0, kq + h)),
            pl.BlockSpec((None, 8, vw), lambda b, h: (b, 0, kq + h)),
            pl.BlockSpec((None, GDN_CONV, qw), lambda b, h: (layer, 0, h)),
            pl.BlockSpec((None, GDN_CONV, qw), lambda b, h: (layer, 0, kq + h)),
            pl.BlockSpec((None, GDN_CONV, vw), lambda b, h: (layer, 0, kq + h)),
            pl.BlockSpec((None, nv, ng8, gw), lambda b, h: (b, h, 0, 0)),
            pl.BlockSpec((None, nv, ng8, gw), lambda b, h: (b, h, 0, 0)),
            pl.BlockSpec((None, nv, 1, gw), lambda b, h: (layer, h, 0, 0)),
            pl.BlockSpec((None, nv, 1, gw), lambda b, h: (layer, h, 0, 0)),
            pl.BlockSpec((None, 1, HD), lambda b, h: (layer, 0, 0)),
            pl.BlockSpec((None, nv, HD, HD), lambda b, h: (b, h, 0, 0)),
        ],
        out_specs=[
            pl.BlockSpec((t_len, vw), lambda b, h: (b, h)),
            pl.BlockSpec((None, nv, HD, HD), lambda b, h: (b, h, 0, 0)),
        ],
        out_shape=[
            jax.ShapeDtypeStruct((batch * t_len, GDN_VAL_DIM), BF16),
            jax.ShapeDtypeStruct((batch, GDN_V_HEADS, HD, HD), F32),
        ],
        scratch_shapes=[
            pltpu.VMEM((t_len, qw), F32),
            pltpu.VMEM((t_len, qw), F32),
            pltpu.VMEM((t_len, vw), F32),
            pltpu.VMEM((nv, t_len, HD), F32),
            pltpu.VMEM((nv, t_len, HD), BF16),
            pltpu.VMEM((nv, t_len, C), BF16),
            pltpu.VMEM((nv, ng8, gw), F32),
            pltpu.VMEM((nv, t_len, HD), F32),
            pltpu.VMEM((nv, t_len, HD), F32),
        ],
        compiler_params=_cparams(("parallel", "arbitrary"), vmem),
        name="gdn_core",
    )(p, p, p, p, prev8, prev8, prev8, conv_w_all, conv_w_all, conv_w_all,
      b_rows, a_rows, alog_b, dtb_b, norm_w_all, s0)


def _diff_lambda(lq1_ref, lk1_ref, lq2_ref, lk2_ref, lam_init):
    s1 = jnp.sum(lq1_ref[...] * lk1_ref[...], axis=-1, keepdims=True)
    s2 = jnp.sum(lq2_ref[...] * lk2_ref[...], axis=-1, keepdims=True)
    return jnp.exp(s1) - jnp.exp(s2) + lam_init


def _flash_body(q_ref, k_ref, v_ref, lq1_ref, lk1_ref, lq2_ref, lk2_ref, sn_ref, o_ref,
                m_s, l_s, acc_s, *, tq, sub, lam_init):
    qi = pl.program_id(2)
    scale = DIFF_HEAD ** -0.5
    q = q_ref[...]
    q_maps = (q[:, :DIFF_HEAD].astype(BF16), q[:, DIFF_HEAD:].astype(BF16))
    m_s[...] = jnp.full(m_s.shape, -jnp.inf, F32)
    l_s[...] = jnp.zeros(l_s.shape, F32)
    acc_s[...] = jnp.zeros(acc_s.shape, F32)
    c2 = scale * math.log2(math.e)
    maps = (0, 1)

    def block(kb, masked):
        r0 = pl.multiple_of(kb * tq, tq)
        k = k_ref[pl.ds(r0, tq), :]
        v = v_ref[pl.ds(r0, tq), :].astype(BF16)
        ks = [k[:, c * DIFF_HEAD:(c + 1) * DIFF_HEAD].astype(BF16) for c in maps]
        chains = [(r, c) for r in range(tq // sub) for c in maps]

        def scores(r, c):
            return _dot_nt(q_maps[c][r * sub:(r + 1) * sub, :], ks[c])

        s_next = scores(*chains[0])
        for i, (r, c) in enumerate(chains):
            s = s_next
            if i + 1 < len(chains):
                s_next = scores(*chains[i + 1])
            rs = slice(r * sub, (r + 1) * sub)
            if masked:
                keep = (lax.broadcasted_iota(jnp.int32, (sub, tq), 1)
                        <= lax.broadcasted_iota(jnp.int32, (sub, tq), 0) + r * sub)
                s = jnp.where(keep, s, -jnp.inf)
            m_prev = m_s[c, rs, :]
            m_new = jnp.maximum(m_prev, jnp.max(s, axis=-1, keepdims=True))
            p = jnp.exp2((s - jnp.concatenate([m_new] * (tq // V7X_LANES), axis=1)) * c2)
            alpha = jnp.exp2((m_prev - m_new) * c2)
            p_lanes = p[:, :V7X_LANES]
            for t in range(1, tq // V7X_LANES):
                p_lanes = p_lanes + p[:, t * V7X_LANES:(t + 1) * V7X_LANES]
            l_s[c, rs, :] = alpha * l_s[c, rs, :] + p_lanes
            acc_s[c, rs, :] = (jnp.concatenate([alpha] * (DIFF_V_HEAD // V7X_LANES), axis=1) * acc_s[c, rs, :]
                               + _dot(p.astype(BF16), v))
            m_s[c, rs, :] = m_new

    def body(kb, carry):
        block(kb, False)
        return carry

    lax.fori_loop(0, qi, body, 0)
    block(qi, True)
    lam = _diff_lambda(lq1_ref, lk1_ref, lq2_ref, lk2_ref, lam_init)
    l0 = jnp.sum(l_s[0], axis=-1, keepdims=True)
    l1 = jnp.sum(l_s[1], axis=-1, keepdims=True)
    o = acc_s[0] / l0 - lam * (acc_s[1] / l1)
    o_ref[...] = (_rms_rows(o) * sn_ref[...] * (1.0 - lam_init)).astype(BF16)


def _flash(qk, v, lq1, lk1, lq2, lk2, sn_all, w_layer, lam_init, batch, t_len, tq):
    nq = t_len // tq
    dv = DIFF_V_HEAD
    vmem = 2 * tq * dv * 4 + 4 * t_len * dv * 4 + 2 * tq * dv * 2 + 2 * tq * dv * 4 + 8 * tq * tq * 4 + 4 * 2**20
    lam_spec = pl.BlockSpec((None, 1, DIFF_HEAD), lambda b, h, i: (w_layer, 0, 0))
    return pl.pallas_call(
        functools.partial(_flash_body, tq=tq, sub=min(FLASH_SUB_ROWS, tq), lam_init=lam_init),
        grid=(batch, DIFF_HEADS, nq),
        in_specs=[
            pl.BlockSpec((None, tq, dv), lambda b, h, i: (0, b * nq + i, h)),
            pl.BlockSpec((None, t_len, dv), lambda b, h, i: (1 + w_layer, b, h)),
            pl.BlockSpec((None, t_len, dv), lambda b, h, i: (w_layer, b, h)),
            lam_spec, lam_spec, lam_spec, lam_spec,
            pl.BlockSpec((None, 1, dv), lambda b, h, i: (w_layer, 0, 0)),
        ],
        out_specs=pl.BlockSpec((tq, dv), lambda b, h, i: (b * nq + i, h)),
        out_shape=jax.ShapeDtypeStruct((batch * t_len, DIFF_V_DIM), BF16),
        scratch_shapes=[
            pltpu.VMEM((2, tq, V7X_LANES), F32),
            pltpu.VMEM((2, tq, V7X_LANES), F32),
            pltpu.VMEM((2, tq, dv), F32),
        ],
        compiler_params=_cparams(("parallel", "parallel", "arbitrary"), vmem),
        name="diff_flash",
    )(qk, qk, v, lq1, lk1, lq2, lk2, sn_all)


def _decode_body(pt_ref, q_ref, kn_ref, vn_ref, *rest, n_steps, pps, n_new, lam_init):
    del pt_ref
    kc_refs, vc_refs = rest[:pps], rest[pps:2 * pps]
    lq1_ref, lk1_ref, lq2_ref, lk2_ref, sn_ref, o_ref, m_s, l_s, acc_s = rest[2 * pps:]
    p = pl.program_id(1)
    scale = DIFF_HEAD ** -0.5
    n_half = q_ref.shape[1]
    n_rows = 2 * n_half
    n_cols = vn_ref.shape[0]

    @pl.when(p == 0)
    def _():
        m_s[...] = jnp.full(m_s.shape, -jnp.inf, F32)
        l_s[...] = jnp.zeros(l_s.shape, F32)
        acc_s[...] = jnp.zeros(acc_s.shape, F32)

    row_head = (lax.broadcasted_iota(jnp.int32, (n_rows, n_cols), 0) % n_half) // n_new
    col = lax.broadcasted_iota(jnp.int32, (n_rows, n_cols), 1)
    head_ok = row_head == col % DIFF_HEADS

    q_bf = [q_ref[c].astype(BF16) for c in range(2)]

    def update(k_refs, v_refs, mask):
        ss = [jnp.concatenate(
            [_dot_nt(q_bf[c], k_ref[pl.ds(c, n_cols, stride=2), :].astype(BF16)) for c in range(2)],
            axis=0) * scale for k_ref in k_refs]
        ss = [jnp.where(mask, s, -jnp.inf) for s in ss]
        m_prev = m_s[...]
        m_new = m_prev
        for s in ss:
            m_new = jnp.maximum(m_new, jnp.max(s, axis=-1, keepdims=True))
        alpha = jnp.exp(m_prev - m_new)
        prs = [jnp.exp(s - m_new) for s in ss]
        l_new = alpha * l_s[...]
        for pr in prs:
            l_new = l_new + jnp.sum(pr, axis=-1, keepdims=True)
        pvs = [_dot(pr.astype(BF16), v_ref[...].astype(BF16)) for pr, v_ref in zip(prs, v_refs)]
        acc = alpha * acc_s[...]
        for pv in pvs:
            acc = acc + pv
        l_s[...] = l_new
        m_s[...] = m_new
        acc_s[...] = acc

    update(kc_refs, vc_refs, head_ok)

    @pl.when(p == n_steps - 1)
    def _():
        qidx = lax.broadcasted_iota(jnp.int32, (n_rows, n_cols), 0) % n_new
        causal = col // DIFF_HEADS <= qidx
        update([kn_ref], [vn_ref], jnp.logical_and(head_ok, causal))
        lam = _diff_lambda(lq1_ref, lk1_ref, lq2_ref, lk2_ref, lam_init)
        o = acc_s[...] / l_s[...]
        od = o[:n_half] - lam * o[n_half:]
        o_ref[...] = _rms_rows(od) * sn_ref[...] * (1.0 - lam_init)


def _decode(page_table, q_maps, kn_pad, vn_pad, cache_k, cache_v, lq1, lk1, lq2, lk2, sn_all,
            w_layer, lam_init, n_new):
    bs, n_pages = page_table.shape
    n_half = q_maps.shape[2]
    k_rows, v_rows = cache_k.shape[2], cache_v.shape[2]
    pps = math.gcd(n_pages, DECODE_PAGES_PER_STEP)
    n_steps = n_pages // pps
    lam_spec = pl.BlockSpec((None, 1, DIFF_HEAD), lambda b, p, pt: (w_layer, 0, 0))
    vmem = 3 * pps * (k_rows * DIFF_HEAD + v_rows * DIFF_V_HEAD) * 4 + 20 * 2**20

    def page_spec(rows, width, i):
        return pl.BlockSpec((None, None, rows, width), lambda b, p, pt: (w_layer, pt[b, p * pps + i], 0, 0))

    grid_spec = pltpu.PrefetchScalarGridSpec(
        num_scalar_prefetch=1,
        grid=(bs, n_steps),
        in_specs=[
            pl.BlockSpec((None, 2, n_half, DIFF_HEAD), lambda b, p, pt: (b, 0, 0, 0)),
            pl.BlockSpec((None, k_rows, DIFF_HEAD), lambda b, p, pt: (b, 0, 0)),
            pl.BlockSpec((None, v_rows, DIFF_V_HEAD), lambda b, p, pt: (b, 0, 0)),
            *[page_spec(k_rows, DIFF_HEAD, i) for i in range(pps)],
            *[page_spec(v_rows, DIFF_V_HEAD, i) for i in range(pps)],
            lam_spec, lam_spec, lam_spec, lam_spec,
            pl.BlockSpec((None, 1, DIFF_V_HEAD), lambda b, p, pt: (w_layer, 0, 0)),
        ],
        out_specs=pl.BlockSpec((None, n_half, DIFF_V_HEAD), lambda b, p, pt: (b, 0, 0)),
        scratch_shapes=[
            pltpu.VMEM((2 * n_half, 1), F32),
            pltpu.VMEM((2 * n_half, 1), F32),
            pltpu.VMEM((2 * n_half, DIFF_V_HEAD), F32),
        ],
    )
    return pl.pallas_call(
        functools.partial(_decode_body, n_steps=n_steps, pps=pps, n_new=n_new, lam_init=lam_init),
        grid_spec=grid_spec,
        out_shape=jax.ShapeDtypeStruct((bs, n_half, DIFF_V_HEAD), F32),
        compiler_params=_cparams(("parallel", "arbitrary"), vmem),
        name="diff_decode",
    )(page_table, q_maps, kn_pad, vn_pad, *([cache_k] * pps), *([cache_v] * pps), lq1, lk1, lq2, lk2, sn_all)


def _rope_tables(pos):
    half = ROPE_DIM // 2
    inv = ROPE_THETA ** (-jnp.arange(half, dtype=F32) * 2.0 / ROPE_DIM)
    ang = pos.astype(F32)[:, None] * inv[None, :]
    cos, sin = jnp.cos(ang), jnp.sin(ang)
    n = pos.shape[0]
    tail = DIFF_HEAD - ROPE_DIM
    c = jnp.concatenate([cos, cos, jnp.ones((n, tail), F32)], axis=1)
    sa = jnp.concatenate([-sin, jnp.zeros((n, half + tail), F32)], axis=1)
    sb = jnp.concatenate([jnp.zeros((n, half), F32), sin, jnp.zeros((n, tail), F32)], axis=1)
    return c, sa, sb


def _gate_rows(ba, batch, t_len):
    gw = min(GDN_SOLVE_WIDTH, t_len)
    ng = t_len // gw
    ba = ba[:, :2 * GDN_V_HEADS]
    ba = ba.reshape(batch, ng, gw, 2, GDN_V_HEADS).transpose(3, 0, 4, 1, 2)
    ng8 = -(-ng // 8) * 8
    ba = jnp.pad(ba, ((0, 0), (0, 0), (0, 0), (0, ng8 - ng), (0, 0)))
    return ba[0], ba[1]


def kernel(x_prompt, x_sample, state_delta, state_conv, cache_k, cache_v, page_table, norm_ffn1, ffn1_w_gu, ffn1_w_dn, norm_mix, norm_ffn2, ffn2_w_gu, ffn2_w_dn, gdn_w_in, gdn_conv_w, gdn_a_log, gdn_dt_bias, gdn_norm_w, gdn_w_out, diff_w_in, diff_q_norm, diff_k_norm, diff_lam_q1, diff_lam_k1, diff_lam_q2, diff_lam_k2, diff_sub_norm, diff_w_out):
    bp, tp, d = x_prompt.shape
    bs, ts, _ = x_sample.shape
    depth = norm_ffn1.shape[0]
    n_pages = page_table.shape[1]
    past = n_pages * PAGE_SIZE
    C = GDN_CHUNK

    row = lambda w: w.reshape(w.shape[0], 1, w.shape[1])
    w_gu1, w_dn1, w_gu2, w_dn2 = ffn1_w_gu, ffn1_w_dn, ffn2_w_gu, ffn2_w_dn
    tn = 512
    tn_gdn = 1024
    gdn_n = GDN_CONV_DIM + GDN_VAL_DIM
    n_ba = 2 * GDN_V_HEADS
    w_gin_t = jnp.swapaxes(gdn_w_in, 1, 2)
    w_gout, w_din, w_dout = gdn_w_out, diff_w_in, diff_w_out
    g_ffn1, g_mix, g_ffn2 = row(norm_ffn1), row(norm_mix), row(norm_ffn2)
    alog_b, dtb_b = gdn_a_log, gdn_dt_bias
    gdn_nw = row(gdn_norm_w)
    qk_n = jnp.stack([diff_q_norm, diff_k_norm], axis=1).reshape(-1, 1, DIFF_HEAD)
    lq1, lk1, lq2, lk2 = row(diff_lam_q1), row(diff_lam_k1), row(diff_lam_q2), row(diff_lam_k2)
    sub_n = row(diff_sub_norm)
    ck = cache_k.reshape(cache_k.shape[0], cache_k.shape[1], PAGE_SIZE * DIFF_HEADS * 2, DIFF_HEAD)
    cv = cache_v.reshape(cache_v.shape[0], cache_v.shape[1], PAGE_SIZE * DIFF_HEADS, DIFF_V_HEAD)

    rope_p = _rope_tables(jnp.arange(tp))
    rope_p = tuple(jnp.tile(t, (bp, 1)) for t in rope_p)
    rope_s = _rope_tables(past + jnp.arange(ts))
    rope_s = tuple(jnp.tile(t, (bs, 1)) for t in rope_s)

    xp = x_prompt.reshape(bp * tp, d)
    xs = x_sample.reshape(bs * ts, d)
    mp, ms = bp * tp, bs * ts
    tm_p, tm_big, tm_s = 1024, 1024, ms
    tf, tf_cast = 512, 256
    ts_pad = -(-ts // C) * C

    prev8_p = jnp.zeros((bp, 8, GDN_CONV_DIM), F32)
    s0_p = jnp.zeros((bp, GDN_V_HEADS, GDN_HEAD, GDN_HEAD), F32)

    p_delta, p_conv = [], []
    n_attn = diff_w_in.shape[0]
    qkp = vp = None
    s_delta, s_conv, s_k, s_v = [], [], [], []
    for i in range(depth):
        xs, wg, wu, wd = _ffn_cast(xs, g_ffn1, w_gu1, w_dn1, i, tf_cast)
        xp = _ffn(xp, g_ffn1, wg, wu, wd, i, tm_p, tf)
        j = i // 2
        if i % 2 == 0:
            pp = _proj(xp, g_mix, w_gin_t, i, j, tm_big, tn_gdn, n_out=gdn_n, transposed=True, out_dtype=BF16)
            ps = _proj(xs, g_mix, w_gin_t, i, j, tm_s, tn_gdn, n_out=gdn_n, transposed=True)
            bap = _proj(xp, g_mix, w_gin_t, i, j, tm_big, n_ba, col0=gdn_n, n_out=n_ba, transposed=True)
            bas = _proj(xs, g_mix, w_gin_t, i, j, tm_s, n_ba, col0=gdn_n, n_out=n_ba, transposed=True)
            b_rows, a_rows = _gate_rows(bap, bp, tp)
            op, sp = _gdn_core(pp, prev8_p, gdn_conv_w, b_rows, a_rows, alog_b, dtb_b, gdn_nw, s0_p, j,
                               bp, tp, tp)
            p_conv.append(pp.reshape(bp, tp, gdn_n)[:, tp - (GDN_CONV - 1):, :GDN_CONV_DIM].astype(F32))
            p_delta.append(sp)
            xp = _outproj(op, w_gout, xp, j, tm_big, tn)

            ps3 = ps.reshape(bs, ts, gdn_n)
            ps_pad = jnp.pad(ps3, ((0, 0), (0, ts_pad - ts), (0, 0))).reshape(bs * ts_pad, gdn_n)
            prev8_s = jnp.pad(state_conv[j], ((0, 0), (8 - (GDN_CONV - 1), 0), (0, 0)))
            bas_pad = jnp.pad(bas.reshape(bs, ts, n_ba), ((0, 0), (0, ts_pad - ts), (0, 0)))
            b_rows, a_rows = _gate_rows(bas_pad.reshape(bs * ts_pad, n_ba), bs, ts_pad)
            os_, ss = _gdn_core(ps_pad, prev8_s, gdn_conv_w, b_rows, a_rows, alog_b, dtb_b, gdn_nw,
                                state_delta[j], j, bs, ts_pad, ts)
            xp_conv = jnp.concatenate([state_conv[j], ps3[:, :, :GDN_CONV_DIM]], axis=1)
            s_conv.append(xp_conv[:, ts:])
            s_delta.append(ss)
            os_ = os_.reshape(bs, ts_pad, GDN_VAL_DIM)[:, :ts].reshape(ms, GDN_VAL_DIM)
            xs = _outproj(os_, w_gout, xs, j, tm_s, tn)
        else:
            lam_init = 0.8 - 0.6 * math.exp(-0.3 * i)
            qkp = _qkproj(xp, g_mix, w_din, qk_n, *rope_p, i, j, tm_big, tn, prev=qkp)
            vp = _proj(xp, g_mix, w_din, i, j, tm_big, tn, col0=2 * DIFF_QK_DIM, n_out=DIFF_V_DIM,
                       planes=n_attn, prev=vp)
            qks = _qkproj(xs, g_mix, w_din, qk_n, *rope_s, i, j, tm_s, tn)
            vs = _proj(xs, g_mix, w_din, i, j, tm_s, tn, col0=2 * DIFF_QK_DIM, n_out=DIFF_V_DIM)
            qs, ks = qks[0], qks[1 + j]
            op = _flash(qkp, vp, lq1, lk1, lq2, lk2, sub_n, j, lam_init, bp, tp, 512)
            xp = _outproj(op, w_dout, xp, j, tm_big, tn)

            q_maps = qs.reshape(bs, ts, DIFF_HEADS, 2, DIFF_HEAD).transpose(0, 3, 2, 1, 4)
            q_maps = q_maps.reshape(bs, 2, DIFF_HEADS * ts, DIFF_HEAD)
            kn_pad = jnp.pad(ks.reshape(bs, ts * DIFF_HEADS * 2, DIFF_HEAD),
                             ((0, 0), (0, (PAGE_SIZE - ts) * DIFF_HEADS * 2), (0, 0)))
            vn_pad = jnp.pad(vs.reshape(bs, ts * DIFF_HEADS, DIFF_V_HEAD),
                             ((0, 0), (0, (PAGE_SIZE - ts) * DIFF_HEADS), (0, 0)))
            od = _decode(page_table, q_maps, kn_pad, vn_pad, ck, cv, lq1, lk1, lq2, lk2, sub_n,
                         j, lam_init, ts)
            od = od.reshape(bs, DIFF_HEADS, ts, DIFF_V_HEAD).transpose(0, 2, 1, 3).reshape(ms, DIFF_V_DIM)
            xs = _outproj(od.astype(BF16), w_dout, xs, j, tm_s, tn)
            s_k.append(ks.reshape(bs, ts, DIFF_HEADS, 2, DIFF_HEAD))
            s_v.append(vs.reshape(bs, ts, DIFF_HEADS, DIFF_V_HEAD))
        xs, wg, wu, wd = _ffn_cast(xs, g_ffn2, w_gu2, w_dn2, i, tf_cast)
        xp = _ffn(xp, g_ffn2, wg, wu, wd, i, tm_p, tf)

    p_k = qkp[1:].reshape(n_attn, bp, tp, DIFF_HEADS, 2, DIFF_HEAD)
    p_v = vp.reshape(n_attn, bp, tp, DIFF_HEADS, DIFF_V_HEAD)
    return (xp.reshape(bp, tp, d), xs.reshape(bs, ts, d),
            jnp.stack(p_delta), jnp.stack(p_conv), p_k, p_v,
            jnp.stack(s_delta), jnp.stack(s_conv), jnp.stack(s_k), jnp.stack(s_v))
```

```python
import functools
import math

import jax
import jax.numpy as jnp
from jax import lax
from jax.experimental import pallas as pl
from jax.experimental.pallas import tpu as pltpu

F32 = jnp.float32
BF16 = jnp.bfloat16
EPS = 1e-6

D_MODEL = 2048
D_FF = 5632
GDN_QK_HEADS = 16
GDN_V_HEADS = 32
GDN_HEAD = 128
GDN_CONV = 4
GDN_CHUNK = 64
GDN_KEY_DIM = GDN_QK_HEADS * GDN_HEAD
GDN_VAL_DIM = GDN_V_HEADS * GDN_HEAD
GDN_CONV_DIM = 2 * GDN_KEY_DIM + GDN_VAL_DIM
GDN_IN_DIM = GDN_CONV_DIM + GDN_VAL_DIM + 2 * GDN_V_HEADS
DIFF_HEADS = 8
DIFF_HEAD = 128
DIFF_V_HEAD = 2 * DIFF_HEAD
DIFF_QK_DIM = DIFF_HEADS * 2 * DIFF_HEAD
DIFF_V_DIM = DIFF_HEADS * DIFF_V_HEAD
ROPE_DIM = DIFF_HEAD // 4
ROPE_THETA = 500000.0
PAGE_SIZE = 128

V7X_LANES = 128
V7X_VMEM_LIMIT_BYTES = 56 * 1024 * 1024

GDN_SOLVE_WIDTH = 128
GDN_SOLVE_INTERLEAVE = 4
GDN_MAX_HEADS_PER_STEP = 4
GDN_CONV_ROWS = 64
GDN_CONV_TILES_PER_TRIP = 32
GDN_STEP_VMEM_BYTES = 44 * 1024 * 1024
DECODE_PAGES_PER_STEP = 8
FLASH_SUB_ROWS = 256


def _cparams(semantics, vmem_bytes):
    return pltpu.CompilerParams(dimension_semantics=semantics,
                                vmem_limit_bytes=int(min(max(vmem_bytes, 16 * 2**20), V7X_VMEM_LIMIT_BYTES)))


def _dot(a, b):
    return jnp.dot(a, b, preferred_element_type=F32)


def _dot_nt(a, b):
    return lax.dot_general(a, b, (((1,), (1,)), ((), ())), preferred_element_type=F32)


def _dot_tn(a, b):
    return lax.dot_general(a, b, (((0,), (0,)), ((), ())), preferred_element_type=F32)


def _rms_rows(x):
    return x * lax.rsqrt(jnp.mean(x * x, axis=-1, keepdims=True) + EPS)


def _silu(x):
    return x * jax.nn.sigmoid(x)


def _lane_sumsq(x):
    ones = jnp.ones((x.shape[1], V7X_LANES), BF16)
    return _dot((x * x).astype(BF16), ones)


def _ffn_prologue(x_ref, g_ref, o_ref, xn_ref):
    @pl.when(pl.program_id(1) == 0)
    def _():
        x = x_ref[...]
        xn_ref[...] = (_rms_rows(x) * g_ref[...]).astype(BF16)
        o_ref[...] = x


def _ffn_body(x_ref, g_ref, wg_ref, wu_ref, wd_ref, o_ref, xn_ref):
    _ffn_prologue(x_ref, g_ref, o_ref, xn_ref)
    xn = xn_ref[...]
    gate = _dot(xn, wg_ref[...])
    up = _dot(xn, wu_ref[...])
    act = (_silu(gate) * up).astype(BF16)
    o_ref[...] += 0.5 * _dot(act, wd_ref[...])


def _ffn_cast_body(x_ref, g_ref, wg_ref, wu_ref, wd_ref, o_ref, wgb_ref, wub_ref, wdb_ref, xn_ref):
    _ffn_prologue(x_ref, g_ref, o_ref, xn_ref)
    wgb_ref[...] = wg_ref[...].astype(BF16)
    wub_ref[...] = wu_ref[...].astype(BF16)
    wdb_ref[...] = wd_ref[...].astype(BF16)
    xn = xn_ref[...]
    gate = _dot(xn, wgb_ref[...])
    up = _dot(xn, wub_ref[...])
    act = (_silu(gate) * up).astype(BF16)
    o_ref[...] += 0.5 * _dot(act, wdb_ref[...])


def _ffn_cast(x, g_all, wgu_all, wdn_all, layer, tf):
    m, d = x.shape
    f = wdn_all.shape[1]
    nf = f // tf
    vmem = 4 * m * d * 4 + 2 * 3 * d * tf * (4 + 2) + 6 * m * tf * 4 + 3 * d * tf * (4 + 2)
    return pl.pallas_call(
        _ffn_cast_body,
        grid=(1, nf),
        in_specs=[
            pl.BlockSpec((m, d), lambda i, j: (0, 0)),
            pl.BlockSpec((None, 1, d), lambda i, j: (layer, 0, 0)),
            pl.BlockSpec((None, d, tf), lambda i, j: (layer, 0, j)),
            pl.BlockSpec((None, d, tf), lambda i, j: (layer, 0, nf + j)),
            pl.BlockSpec((None, tf, d), lambda i, j: (layer, j, 0)),
        ],
        out_specs=[
            pl.BlockSpec((m, d), lambda i, j: (0, 0)),
            pl.BlockSpec((d, tf), lambda i, j: (0, j)),
            pl.BlockSpec((d, tf), lambda i, j: (0, j)),
            pl.BlockSpec((tf, d), lambda i, j: (j, 0)),
        ],
        out_shape=[
            jax.ShapeDtypeStruct((m, d), F32),
            jax.ShapeDtypeStruct((d, f), BF16),
            jax.ShapeDtypeStruct((d, f), BF16),
            jax.ShapeDtypeStruct((f, d), BF16),
        ],
        scratch_shapes=[pltpu.VMEM((m, d), BF16)],
        compiler_params=_cparams(("arbitrary", "arbitrary"), vmem),
        name="half_ffn_cast",
    )(x, g_all, wgu_all, wgu_all, wdn_all)


def _ffn(x, g_all, wg, wu, wd, layer, tm, tf):
    m, d = x.shape
    f = wd.shape[0]
    vmem = 4 * tm * d * 4 + tm * d * 2 + 2 * 3 * d * tf * 2 + 6 * tm * tf * 4
    return pl.pallas_call(
        _ffn_body,
        grid=(m // tm, f // tf),
        in_specs=[
            pl.BlockSpec((tm, d), lambda i, j: (i, 0)),
            pl.BlockSpec((None, 1, d), lambda i, j: (layer, 0, 0)),
            pl.BlockSpec((d, tf), lambda i, j: (0, j)),
            pl.BlockSpec((d, tf), lambda i, j: (0, j)),
            pl.BlockSpec((tf, d), lambda i, j: (j, 0)),
        ],
        out_specs=pl.BlockSpec((tm, d), lambda i, j: (i, 0)),
        out_shape=jax.ShapeDtypeStruct((m, d), F32),
        scratch_shapes=[pltpu.VMEM((tm, d), BF16)],
        compiler_params=_cparams(("parallel", "arbitrary"), vmem),
        name="half_ffn",
    )(x, g_all, wg, wu, wd)


def _proj_body(x_ref, g_ref, w_ref, *rest, transposed):
    o_ref, xn_ref = rest[-2:]

    @pl.when(pl.program_id(1) == 0)
    def _():
        xn_ref[...] = (_rms_rows(x_ref[...]) * g_ref[...]).astype(BF16)

    w = w_ref[...].astype(BF16)
    acc = _dot_nt(xn_ref[...], w) if transposed else _dot(xn_ref[...], w)
    o_ref[...] = acc.astype(o_ref.dtype)


def _proj(x, g_all, w_all, layer, w_layer, tm, tn, col0=0, n_out=None, transposed=False, out_dtype=F32,
          planes=None, prev=None):
    m, d = x.shape
    n_all = w_all.shape[1] if transposed else w_all.shape[2]
    n_out = n_all - col0 if n_out is None else n_out
    jb = col0 // tn
    vmem = 4 * tm * d * 4 + tm * d * 2 + 2 * d * tn * w_all.dtype.itemsize + 4 * tm * tn * 4 + d * tn * 2
    if transposed:
        w_spec = pl.BlockSpec((None, tn, d), lambda i, j: (w_layer, jb + j, 0))
    else:
        w_spec = pl.BlockSpec((None, d, tn), lambda i, j: (w_layer, 0, jb + j))
    in_specs = [
        pl.BlockSpec((tm, d), lambda i, j: (i, 0)),
        pl.BlockSpec((None, 1, d), lambda i, j: (layer, 0, 0)),
        w_spec,
    ]
    args = [x, g_all, w_all]
    aliases = {}
    if planes is None:
        out_spec = pl.BlockSpec((tm, tn), lambda i, j: (i, j))
        out_shape = jax.ShapeDtypeStruct((m, n_out), out_dtype)
    else:
        out_spec = pl.BlockSpec((None, tm, tn), lambda i, j: (w_layer, i, j))
        out_shape = jax.ShapeDtypeStruct((planes, m, n_out), out_dtype)
        if prev is not None:
            in_specs.append(pl.BlockSpec(memory_space=pl.ANY))
            args.append(prev)
            aliases = {len(args) - 1: 0}
    return pl.pallas_call(
        functools.partial(_proj_body, transposed=transposed),
        grid=(m // tm, n_out // tn),
        in_specs=in_specs,
        out_specs=out_spec,
        out_shape=out_shape,
        input_output_aliases=aliases,
        scratch_shapes=[pltpu.VMEM((tm, d), BF16)],
        compiler_params=_cparams(("parallel", "arbitrary"), vmem),
        name="norm_proj",
    )(*args)


def _qkproj_body(x_ref, g_ref, w_ref, nw_ref, cos_ref, sa_ref, sb_ref, *rest):
    o_ref, xn_ref = rest[-2:]

    @pl.when(pl.program_id(1) == 0)
    def _():
        xn_ref[...] = (_rms_rows(x_ref[...]) * g_ref[...]).astype(BF16)

    xn = xn_ref[...]
    wn = nw_ref[...]
    cos, sa, sb = cos_ref[...], sa_ref[...], sb_ref[...]
    pair = 2 * DIFF_HEAD
    n_pairs = o_ref.shape[1] // pair

    def mm(g):
        return _dot(xn, w_ref[:, g * pair:(g + 1) * pair].astype(BF16))

    acc_next = mm(0)
    for g in range(n_pairs):
        acc = acc_next
        if g + 1 < n_pairs:
            acc_next = mm(g + 1)
        for h in range(2):
            y = acc[:, h * DIFF_HEAD:(h + 1) * DIFF_HEAD]
            y = y * lax.rsqrt(_lane_sumsq(y) * (1.0 / DIFF_HEAD) + EPS) * wn
            y = (y * cos + pltpu.roll(y, DIFF_HEAD - ROPE_DIM // 2, 1) * sa
                 + pltpu.roll(y, ROPE_DIM // 2, 1) * sb)
            c0 = g * pair + h * DIFF_HEAD
            o_ref[:, c0:c0 + DIFF_HEAD] = y


def _qkproj(x, g_all, w_all, qk_norm, cos, sa, sb, layer, w_layer, tm, tn, prev=None):
    m, d = x.shape
    nq = DIFF_QK_DIM // tn
    planes = 1 + w_all.shape[0]
    vmem = (4 * tm * d * 4 + tm * d * 2 + 2 * d * tn * w_all.dtype.itemsize + 6 * tm * tn * 4
            + 6 * tm * 128 * 4 + d * tn * 2)
    in_specs = [
        pl.BlockSpec((tm, d), lambda i, j: (i, 0)),
        pl.BlockSpec((None, 1, d), lambda i, j: (layer, 0, 0)),
        pl.BlockSpec((None, d, tn), lambda i, j: (w_layer, 0, j)),
        pl.BlockSpec((None, 1, DIFF_HEAD), lambda i, j: (2 * w_layer + j // nq, 0, 0)),
        pl.BlockSpec((tm, DIFF_HEAD), lambda i, j: (i, 0)),
        pl.BlockSpec((tm, DIFF_HEAD), lambda i, j: (i, 0)),
        pl.BlockSpec((tm, DIFF_HEAD), lambda i, j: (i, 0)),
    ]
    args = [x, g_all, w_all, qk_norm, cos, sa, sb]
    aliases = {}
    if prev is not None:
        in_specs.append(pl.BlockSpec(memory_space=pl.ANY))
        args.append(prev)
        aliases = {len(args) - 1: 0}
    return pl.pallas_call(
        _qkproj_body,
        grid=(m // tm, 2 * nq),
        in_specs=in_specs,
        out_specs=pl.BlockSpec((None, tm, tn), lambda i, j: ((j // nq) * (1 + w_layer), i, j % nq)),
        out_shape=jax.ShapeDtypeStruct((planes, m, DIFF_QK_DIM), F32),
        input_output_aliases=aliases,
        scratch_shapes=[pltpu.VMEM((tm, d), BF16)],
        compiler_params=_cparams(("parallel", "arbitrary"), vmem),
        name="diff_qk_proj",
    )(*args)


def _outproj_body(y_ref, w_ref, x_ref, o_ref):
    o_ref[...] = x_ref[...] + _dot(y_ref[...], w_ref[...].astype(BF16))


def _outproj(y, w_all, x, w_layer, tm, tn):
    m, k = y.shape
    d = x.shape[1]
    vmem = 2 * tm * k * 2 + 2 * k * tn * w_all.dtype.itemsize + 6 * tm * tn * 4 + k * tn * 2
    return pl.pallas_call(
        _outproj_body,
        grid=(m // tm, d // tn),
        in_specs=[
            pl.BlockSpec((tm, k), lambda i, j: (i, 0)),
            pl.BlockSpec((None, k, tn), lambda i, j: (w_layer, 0, j)),
            pl.BlockSpec((tm, tn), lambda i, j: (i, j)),
        ],
        out_specs=pl.BlockSpec((tm, tn), lambda i, j: (i, j)),
        out_shape=jax.ShapeDtypeStruct((m, d), F32),
        compiler_params=_cparams(("parallel", "arbitrary"), vmem),
        name="out_proj",
    )(y, w_all, x)


def _unit_lower_inverse_minus_identity(mms, level_masks, base_mask):
    ys = [-(mm * base_mask) for mm in mms]
    for mask in level_masks:
        os_ = [mm * mask for mm in mms]
        y_bf = [y.astype(BF16) for y in ys]
        zs = [o + _dot(yb, o.astype(BF16)) for o, yb in zip(os_, y_bf)]
        ys = [y - z - _dot(z.astype(BF16), yb) for y, z, yb in zip(ys, zs, y_bf)]
    return ys


def _gdn_body(qp_ref, kp_ref, vp_ref, z_ref, q8_ref, k8_ref, v8_ref, cwq_ref, cwk_ref, cwv_ref,
              b_ref, a_ref, alog_ref, dtb_ref, nw_ref, s0_ref,
              o_ref, sout_ref,
              q_s, k_s, v_s, u_s, w_s, at_s, grow_s, gcc_s, bc_s,
              *, t_len, t_valid, gw, ilv):
    C = gw
    HD = GDN_HEAD
    nc = t_len // C
    ng = t_len // gw
    cpg = gw // C
    nh = q_s.shape[1] // HD
    heads = tuple(range(2 * nh))
    rt = min(GDN_CONV_ROWS, t_len)

    def conv_block(x_ref, p8_ref, cw_ref, r0, cs, first):
        head = p8_ref[:, cs] if first else x_ref[pl.ds(r0 - 16, 16), cs].astype(F32)[8:, :]
        cur = x_ref[pl.ds(r0, rt), cs].astype(F32)
        ext = jnp.concatenate([head, cur], axis=0)
        w = cw_ref[:, cs]
        y = ext[5:5 + rt, :] * w[0:1, :]
        y = y + ext[6:6 + rt, :] * w[1:2, :]
        y = y + ext[7:7 + rt, :] * w[2:3, :]
        y = y + cur * w[3:4, :]
        y = _silu(y)
        if t_valid < t_len:
            rows = r0 + lax.broadcasted_iota(jnp.int32, y.shape, 0)
            y = jnp.where(rows < t_valid, y, 0.0)
        return y

    def l2n(x):
        return x * lax.rsqrt(jnp.sum(x * x, axis=-1, keepdims=True) + EPS)

    def conv_rows(r0, first):
        for qh in range(nh):
            cs = slice(qh * HD, (qh + 1) * HD)
            q_s[pl.ds(r0, rt), cs] = l2n(conv_block(qp_ref, q8_ref, cwq_ref, r0, cs, first)) * (HD ** -0.5)
            k_s[pl.ds(r0, rt), cs] = l2n(conv_block(kp_ref, k8_ref, cwk_ref, r0, cs, first))
        for h in heads:
            cs = slice(h * HD, (h + 1) * HD)
            v_s[pl.ds(r0, rt), cs] = conv_block(vp_ref, v8_ref, cwv_ref, r0, cs, first)

    per_trip = math.gcd(t_len // rt, GDN_CONV_TILES_PER_TRIP)
    conv_rows(0, True)
    for s in range(1, per_trip):
        conv_rows(s * rt, False)

    def conv_body(r, carry):
        for s in range(per_trip):
            conv_rows(pl.multiple_of((r * per_trip + s) * rt, rt), False)
        return carry

    lax.fori_loop(1, t_len // (rt * per_trip), conv_body, 0)

    ri = lax.broadcasted_iota(jnp.int32, (gw, gw), 0)
    ci = lax.broadcasted_iota(jnp.int32, (gw, gw), 1)
    chunk_shift = C.bit_length() - 1
    same_chunk = (ri >> chunk_shift) == (ci >> chunk_shift)
    lower = jnp.logical_and(same_chunk, ci <= ri)
    strict = jnp.logical_and(same_chunk, ci < ri)
    upper_ones = jnp.where(jnp.logical_and(same_chunk, ri <= ci), 1.0, 0.0)
    eye = jnp.where(ri == ci, 1.0, 0.0)
    hi = lax.Precision.HIGHEST
    for h in heads:
        beta = jax.nn.sigmoid(b_ref[h])
        g = -jnp.exp(alog_ref[h]) * jax.nn.softplus(a_ref[h] + dtb_ref[h])
        if t_valid < t_len:
            pos = (lax.broadcasted_iota(jnp.int32, g.shape, 0) * gw
                   + lax.broadcasted_iota(jnp.int32, g.shape, 1))
            g = jnp.where(pos < t_valid, g, 0.0)
            beta = jnp.where(pos < t_valid, beta, 0.0)
        gc = jnp.dot(g, upper_ones, precision=hi, preferred_element_type=F32)
        grow_s[h] = gc
        gc_t = lax.dot_general(eye, gc, (((1,), (1,)), ((), ())), precision=hi,
                               preferred_element_type=F32)
        beta_t = lax.dot_general(eye, beta, (((1,), (1,)), ((), ())), precision=hi,
                                 preferred_element_type=F32)
        for i in range(ng):
            gcc_s[h, i * gw:(i + 1) * gw, :] = jnp.broadcast_to(gc_t[:, i:i + 1], (gw, HD))
            bc_s[h, i * gw:(i + 1) * gw, :] = jnp.broadcast_to(beta_t[:, i:i + 1], (gw, HD))

    base_mask = jnp.where((ri >> 1) == (ci >> 1), 1.0, 0.0)
    level_masks = []
    for s in range(1, chunk_shift):
        same_big = (ri >> (s + 1)) == (ci >> (s + 1))
        same_small = (ri >> s) == (ci >> s)
        level_masks.append(jnp.where(same_big, 1.0, 0.0) - jnp.where(same_small, 1.0, 0.0))

    def lane_tile(x):
        if gw <= HD:
            return x[:, :gw]
        return jnp.concatenate([x] * (gw // HD), axis=1)

    def solve_groups(it, carry):
        chains = []
        for gg in range(ilv):
            gi = it * ilv + gg
            r0 = pl.multiple_of(gi * gw, gw)
            for qh in range(nh):
                kc = k_s[pl.ds(r0, gw), qh * HD:(qh + 1) * HD]
                k_bf = kc.astype(BF16)
                qk = _dot_nt(q_s[pl.ds(r0, gw), qh * HD:(qh + 1) * HD].astype(BF16), k_bf)
                for h in (2 * qh, 2 * qh + 1):
                    gcc = gcc_s[h, pl.ds(r0, gw), :]
                    gd = lane_tile(gcc) - grow_s[h, pl.ds(gi, 1), :]
                    dec = jnp.where(lower, jnp.exp(jnp.where(lower, gd, 0.0)), 0.0)
                    beta_c = bc_s[h, pl.ds(r0, gw), :]
                    chains.append((h, r0, kc * beta_c, k_bf, dec, gcc, beta_c, qk))
        mms = [jnp.where(strict, _dot_nt(kb.astype(BF16), k_bf) * dec, 0.0)
               for (_, _, kb, k_bf, dec, _, _, _) in chains]
        ys = _unit_lower_inverse_minus_identity(mms, level_masks, base_mask)
        rhss = [jnp.concatenate([v_s[pl.ds(r0, gw), h * HD:(h + 1) * HD] * beta_c, kb * jnp.exp(gcc)], axis=1)
                for (h, r0, kb, _, _, gcc, beta_c, _) in chains]
        uws = [rhs + _dot(y.astype(BF16), rhs.astype(BF16)) for rhs, y in zip(rhss, ys)]
        for (h, r0, _, _, dec, _, _, qk), uw in zip(chains, uws):
            u_s[h, pl.ds(r0, gw), :] = uw[:, :HD]
            w_s[h, pl.ds(r0, gw), :] = uw[:, HD:].astype(BF16)
            at_s[h, pl.ds(r0, gw), :] = (qk * dec).astype(BF16)
        return carry

    lax.fori_loop(0, ng // ilv, solve_groups, 0)

    nw = nw_ref[...]

    def step(c, states):
        r0 = pl.multiple_of(c * C, C)
        kcs = [k_s[pl.ds(r0, C), qh * HD:(qh + 1) * HD] for qh in range(nh)]
        qcs = [q_s[pl.ds(r0, C), qh * HD:(qh + 1) * HD] for qh in range(nh)]
        gccs = [gcc_s[h, pl.ds(r0, C), :] for h in heads]
        lhss = [jnp.concatenate([w_s[h, pl.ds(r0, C), :], (qcs[h // 2] * jnp.exp(gccs[h])).astype(BF16)], axis=0)
                for h in heads]
        wss = [_dot(lhss[h], states[h].astype(BF16)) for h in heads]
        vns = [(u_s[h, pl.ds(r0, C), :] - wss[h][:C]).astype(BF16) for h in heads]
        glast = [gccs[h][C - 1:C, :] for h in heads]
        kes = [(kcs[h // 2] * jnp.exp(glast[h] - gccs[h])).astype(BF16) for h in heads]
        outs = [wss[h][C:] + _dot(at_s[h, pl.ds(r0, C), :], vns[h]) for h in heads]
        new_states = [states[h] * jnp.exp(glast[h]) + _dot_tn(kes[h], vns[h]) for h in heads]
        for h in heads:
            zc = z_ref[pl.ds(r0, C), h * HD:(h + 1) * HD].astype(F32)
            o_ref[pl.ds(r0, C), h * HD:(h + 1) * HD] = (_rms_rows(outs[h]) * nw * _silu(zc)).astype(BF16)
        return tuple(new_states)

    s_fin = lax.fori_loop(0, nc, step, tuple(s0_ref[h] for h in heads))
    for h in heads:
        sout_ref[h] = s_fin[h]


def _gdn_core(p, prev8, conv_w_all, b_rows, a_rows, alog_b, dtb_b, norm_w_all, s0, layer,
              batch, t_len, t_valid):
    HD = GDN_HEAD
    ng8, gw = b_rows.shape[2], b_rows.shape[3]
    C = gw
    ng = t_len // gw
    alog_b = jnp.broadcast_to(alog_b[:, :, None, None], alog_b.shape + (1, gw))
    dtb_b = jnp.broadcast_to(dtb_b[:, :, None, None], dtb_b.shape + (1, gw))
    step_bytes = (12 * p.dtype.itemsize + 56) * t_len * HD
    nh = GDN_MAX_HEADS_PER_STEP
    while nh > 1 and step_bytes * nh > GDN_STEP_VMEM_BYTES:
        nh //= 2
    ilv = math.gcd(ng, max(1, GDN_SOLVE_INTERLEAVE // nh))
    qw, vw, nv = nh * HD, 2 * nh * HD, 2 * nh
    kq = GDN_KEY_DIM // qw
    kz = GDN_CONV_DIM // vw
    vmem = step_bytes * nh + 16 * 2**20
    grid = (batch, GDN_QK_HEADS // nh)
    return pl.pallas_call(
        functools.partial(_gdn_body, t_len=t_len, t_valid=t_valid, gw=gw, ilv=ilv),
        grid=grid,
        in_specs=[
            pl.BlockSpec((t_len, qw), lambda b, h: (b, h)),
            pl.BlockSpec((t_len, qw), lambda b, h: (b, kq + h)),
            pl.BlockSpec((t_len, vw), lambda b, h: (b, kq + h)),
            pl.BlockSpec((t_len, vw), lambda b, h: (b, kz + h)),
            pl.BlockSpec((None, 8, qw), lambda b, h: (b, 0, h)),
            pl.BlockSpec((None, 8, qw), lambda b, h: (b, 0, kq + h)),
            pl.BlockSpec((None, 8, vw), lambda b, h: (b, 0, kq + h)),
            pl.BlockSpec((None, GDN_CONV, qw), lambda b, h: (layer, 0, h)),
            pl.BlockSpec((None, GDN_CONV, qw), lambda b, h: (layer, 0, kq + h)),
            pl.BlockSpec((None, GDN_CONV, vw), lambda b, h: (layer, 0, kq + h)),
            pl.BlockSpec((None, nv, ng8, gw), lambda b, h: (b, h, 0, 0)),
            pl.BlockSpec((None, nv, ng8, gw), lambda b, h: (b, h, 0, 0)),
            pl.BlockSpec((None, nv, 1, gw), lambda b, h: (layer, h, 0, 0)),
            pl.BlockSpec((None, nv, 1, gw), lambda b, h: (layer, h, 0, 0)),
            pl.BlockSpec((None, 1, HD), lambda b, h: (layer, 0, 0)),
            pl.BlockSpec((None, nv, HD, HD), lambda b, h: (b, h, 0, 0)),
        ],
        out_specs=[
            pl.BlockSpec((t_len, vw), lambda b, h: (b, h)),
            pl.BlockSpec((None, nv, HD, HD), lambda b, h: (b, h, 0, 0)),
        ],
        out_shape=[
            jax.ShapeDtypeStruct((batch * t_len, GDN_VAL_DIM), BF16),
            jax.ShapeDtypeStruct((batch, GDN_V_HEADS, HD, HD), F32),
        ],
        scratch_shapes=[
            pltpu.VMEM((t_len, qw), F32),
            pltpu.VMEM((t_len, qw), F32),
            pltpu.VMEM((t_len, vw), F32),
            pltpu.VMEM((nv, t_len, HD), F32),
            pltpu.VMEM((nv, t_len, HD), BF16),
            pltpu.VMEM((nv, t_len, C), BF16),
            pltpu.VMEM((nv, ng8, gw), F32),
            pltpu.VMEM((nv, t_len, HD), F32),
            pltpu.VMEM((nv, t_len, HD), F32),
        ],
        compiler_params=_cparams(("parallel", "arbitrary"), vmem),
        name="gdn_core",
    )(p, p, p, p, prev8, prev8, prev8, conv_w_all, conv_w_all, conv_w_all,
      b_rows, a_rows, alog_b, dtb_b, norm_w_all, s0)


def _diff_lambda(lq1_ref, lk1_ref, lq2_ref, lk2_ref, lam_init):
    s1 = jnp.sum(lq1_ref[...] * lk1_ref[...], axis=-1, keepdims=True)
    s2 = jnp.sum(lq2_ref[...] * lk2_ref[...], axis=-1, keepdims=True)
    return jnp.exp(s1) - jnp.exp(s2) + lam_init


def _flash_body(q_ref, k_ref, v_ref, lq1_ref, lk1_ref, lq2_ref, lk2_ref, sn_ref, o_ref,
                m_s, l_s, acc_s, *, tq, sub, lam_init):
    qi = pl.program_id(2)
    scale = DIFF_HEAD ** -0.5
    q = q_ref[...]
    q_maps = (q[:, :DIFF_HEAD].astype(BF16), q[:, DIFF_HEAD:].astype(BF16))
    m_s[...] = jnp.full(m_s.shape, -jnp.inf, F32)
    l_s[...] = jnp.zeros(l_s.shape, F32)
    acc_s[...] = jnp.zeros(acc_s.shape, F32)
    c2 = scale * math.log2(math.e)
    maps = (0, 1)

    def block(kb, masked):
        r0 = pl.multiple_of(kb * tq, tq)
        k = k_ref[pl.ds(r0, tq), :]
        v = v_ref[pl.ds(r0, tq), :].astype(BF16)
        ks = [k[:, c * DIFF_HEAD:(c + 1) * DIFF_HEAD].astype(BF16) for c in maps]
        chains = [(r, c) for r in range(tq // sub) for c in maps]

        def scores(r, c):
            return _dot_nt(q_maps[c][r * sub:(r + 1) * sub, :], ks[c])

        s_next = scores(*chains[0])
        for i, (r, c) in enumerate(chains):
            s = s_next
            if i + 1 < len(chains):
                s_next = scores(*chains[i + 1])
            rs = slice(r * sub, (r + 1) * sub)
            if masked:
                keep = (lax.broadcasted_iota(jnp.int32, (sub, tq), 1)
                        <= lax.broadcasted_iota(jnp.int32, (sub, tq), 0) + r * sub)
                s = jnp.where(keep, s, -jnp.inf)
            m_prev = m_s[c, rs, :]
            m_new = jnp.maximum(m_prev, jnp.max(s, axis=-1, keepdims=True))
            p = jnp.exp2((s - jnp.concatenate([m_new] * (tq // V7X_LANES), axis=1)) * c2)
            alpha = jnp.exp2((m_prev - m_new) * c2)
            p_lanes = p[:, :V7X_LANES]
            for t in range(1, tq // V7X_LANES):
                p_lanes = p_lanes + p[:, t * V7X_LANES:(t + 1) * V7X_LANES]
            l_s[c, rs, :] = alpha * l_s[c, rs, :] + p_lanes
            acc_s[c, rs, :] = (jnp.concatenate([alpha] * (DIFF_V_HEAD // V7X_LANES), axis=1) * acc_s[c, rs, :]
                               + _dot(p.astype(BF16), v))
            m_s[c, rs, :] = m_new

    def body(kb, carry):
        block(kb, False)
        return carry

    lax.fori_loop(0, qi, body, 0)
    block(qi, True)
    lam = _diff_lambda(lq1_ref, lk1_ref, lq2_ref, lk2_ref, lam_init)
    l0 = jnp.sum(l_s[0], axis=-1, keepdims=True)
    l1 = jnp.sum(l_s[1], axis=-1, keepdims=True)
    o = acc_s[0] / l0 - lam * (acc_s[1] / l1)
    o_ref[...] = (_rms_rows(o) * sn_ref[...] * (1.0 - lam_init)).astype(BF16)


def _flash(qk, v, lq1, lk1, lq2, lk2, sn_all, w_layer, lam_init, batch, t_len, tq):
    nq = t_len // tq
    dv = DIFF_V_HEAD
    vmem = 2 * tq * dv * 4 + 4 * t_len * dv * 4 + 2 * tq * dv * 2 + 2 * tq * dv * 4 + 8 * tq * tq * 4 + 4 * 2**20
    lam_spec = pl.BlockSpec((None, 1, DIFF_HEAD), lambda b, h, i: (w_layer, 0, 0))
    return pl.pallas_call(
        functools.partial(_flash_body, tq=tq, sub=min(FLASH_SUB_ROWS, tq), lam_init=lam_init),
        grid=(batch, DIFF_HEADS, nq),
        in_specs=[
            pl.BlockSpec((None, tq, dv), lambda b, h, i: (0, b * nq + i, h)),
            pl.BlockSpec((None, t_len, dv), lambda b, h, i: (1 + w_layer, b, h)),
            pl.BlockSpec((None, t_len, dv), lambda b, h, i: (w_layer, b, h)),
            lam_spec, lam_spec, lam_spec, lam_spec,
            pl.BlockSpec((None, 1, dv), lambda b, h, i: (w_layer, 0, 0)),
        ],
        out_specs=pl.BlockSpec((tq, dv), lambda b, h, i: (b * nq + i, h)),
        out_shape=jax.ShapeDtypeStruct((batch * t_len, DIFF_V_DIM), BF16),
        scratch_shapes=[
            pltpu.VMEM((2, tq, V7X_LANES), F32),
            pltpu.VMEM((2, tq, V7X_LANES), F32),
            pltpu.VMEM((2, tq, dv), F32),
        ],
        compiler_params=_cparams(("parallel", "parallel", "arbitrary"), vmem),
        name="diff_flash",
    )(qk, qk, v, lq1, lk1, lq2, lk2, sn_all)


def _decode_body(pt_ref, q_ref, kn_ref, vn_ref, *rest, n_steps, pps, n_new, lam_init):
    del pt_ref
    kc_refs, vc_refs = rest[:pps], rest[pps:2 * pps]
    lq1_ref, lk1_ref, lq2_ref, lk2_ref, sn_ref, o_ref, m_s, l_s, acc_s = rest[2 * pps:]
    p = pl.program_id(1)
    scale = DIFF_HEAD ** -0.5
    n_half = q_ref.shape[1]
    n_rows = 2 * n_half
    n_cols = vn_ref.shape[0]

    @pl.when(p == 0)
    def _():
        m_s[...] = jnp.full(m_s.shape, -jnp.inf, F32)
        l_s[...] = jnp.zeros(l_s.shape, F32)
        acc_s[...] = jnp.zeros(acc_s.shape, F32)

    row_head = (lax.broadcasted_iota(jnp.int32, (n_rows, n_cols), 0) % n_half) // n_new
    col = lax.broadcasted_iota(jnp.int32, (n_rows, n_cols), 1)
    head_ok = row_head == col % DIFF_HEADS

    q_bf = [q_ref[c].astype(BF16) for c in range(2)]

    def update(k_refs, v_refs, mask):
        ss = [jnp.concatenate(
            [_dot_nt(q_bf[c], k_ref[pl.ds(c, n_cols, stride=2), :].astype(BF16)) for c in range(2)],
            axis=0) * scale for k_ref in k_refs]
        ss = [jnp.where(mask, s, -jnp.inf) for s in ss]
        m_prev = m_s[...]
        m_new = m_prev
        for s in ss:
            m_new = jnp.maximum(m_new, jnp.max(s, axis=-1, keepdims=True))
        alpha = jnp.exp(m_prev - m_new)
        prs = [jnp.exp(s - m_new) for s in ss]
        l_new = alpha * l_s[...]
        for pr in prs:
            l_new = l_new + jnp.sum(pr, axis=-1, keepdims=True)
        pvs = [_dot(pr.astype(BF16), v_ref[...].astype(BF16)) for pr, v_ref in zip(prs, v_refs)]
        acc = alpha * acc_s[...]
        for pv in pvs:
            acc = acc + pv
        l_s[...] = l_new
        m_s[...] = m_new
        acc_s[...] = acc

    update(kc_refs, vc_refs, head_ok)

    @pl.when(p == n_steps - 1)
    def _():
        qidx = lax.broadcasted_iota(jnp.int32, (n_rows, n_cols), 0) % n_new
        causal = col // DIFF_HEADS <= qidx
        update([kn_ref], [vn_ref], jnp.logical_and(head_ok, causal))
        lam = _diff_lambda(lq1_ref, lk1_ref, lq2_ref, lk2_ref, lam_init)
        o = acc_s[...] / l_s[...]
        od = o[:n_half] - lam * o[n_half:]
        o_ref[...] = _rms_rows(od) * sn_ref[...] * (1.0 - lam_init)


def _decode(page_table, q_maps, kn_pad, vn_pad, cache_k, cache_v, lq1, lk1, lq2, lk2, sn_all,
            w_layer, lam_init, n_new):
    bs, n_pages = page_table.shape
    n_half = q_maps.shape[2]
    k_rows, v_rows = cache_k.shape[2], cache_v.shape[2]
    pps = math.gcd(n_pages, DECODE_PAGES_PER_STEP)
    n_steps = n_pages // pps
    lam_spec = pl.BlockSpec((None, 1, DIFF_HEAD), lambda b, p, pt: (w_layer, 0, 0))
    vmem = 3 * pps * (k_rows * DIFF_HEAD + v_rows * DIFF_V_HEAD) * 4 + 20 * 2**20

    def page_spec(rows, width, i):
        return pl.BlockSpec((None, None, rows, width), lambda b, p, pt: (w_layer, pt[b, p * pps + i], 0, 0))

    grid_spec = pltpu.PrefetchScalarGridSpec(
        num_scalar_prefetch=1,
        grid=(bs, n_steps),
        in_specs=[
            pl.BlockSpec((None, 2, n_half, DIFF_HEAD), lambda b, p, pt: (b, 0, 0, 0)),
            pl.BlockSpec((None, k_rows, DIFF_HEAD), lambda b, p, pt: (b, 0, 0)),
            pl.BlockSpec((None, v_rows, DIFF_V_HEAD), lambda b, p, pt: (b, 0, 0)),
            *[page_spec(k_rows, DIFF_HEAD, i) for i in range(pps)],
            *[page_spec(v_rows, DIFF_V_HEAD, i) for i in range(pps)],
            lam_spec, lam_spec, lam_spec, lam_spec,
            pl.BlockSpec((None, 1, DIFF_V_HEAD), lambda b, p, pt: (w_layer, 0, 0)),
        ],
        out_specs=pl.BlockSpec((None, n_half, DIFF_V_HEAD), lambda b, p, pt: (b, 0, 0)),
        scratch_shapes=[
            pltpu.VMEM((2 * n_half, 1), F32),
            pltpu.VMEM((2 * n_half, 1), F32),
            pltpu.VMEM((2 * n_half, DIFF_V_HEAD), F32),
        ],
    )
    return pl.pallas_call(
        functools.partial(_decode_body, n_steps=n_steps, pps=pps, n_new=n_new, lam_init=lam_init),
        grid_spec=grid_spec,
        out_shape=jax.ShapeDtypeStruct((bs, n_half, DIFF_V_HEAD), F32),
        compiler_params=_cparams(("parallel", "arbitrary"), vmem),
        name="diff_decode",
    )(page_table, q_maps, kn_pad, vn_pad, *([cache_k] * pps), *([cache_v] * pps), lq1, lk1, lq2, lk2, sn_all)


def _rope_tables(pos):
    half = ROPE_DIM // 2
    inv = ROPE_THETA ** (-jnp.arange(half, dtype=F32) * 2.0 / ROPE_DIM)
    ang = pos.astype(F32)[:, None] * inv[None, :]
    cos, sin = jnp.cos(ang), jnp.sin(ang)
    n = pos.shape[0]
    tail = DIFF_HEAD - ROPE_DIM
    c = jnp.concatenate([cos, cos, jnp.ones((n, tail), F32)], axis=1)
    sa = jnp.concatenate([-sin, jnp.zeros((n, half + tail), F32)], axis=1)
    sb = jnp.concatenate([jnp.zeros((n, half), F32), sin, jnp.zeros((n, tail), F32)], axis=1)
    return c, sa, sb


def _gate_rows(ba, batch, t_len):
    gw = min(GDN_SOLVE_WIDTH, t_len)
    ng = t_len // gw
    ba = ba[:, :2 * GDN_V_HEADS]
    ba = ba.reshape(batch, ng, gw, 2, GDN_V_HEADS).transpose(3, 0, 4, 1, 2)
    ng8 = -(-ng // 8) * 8
    ba = jnp.pad(ba, ((0, 0), (0, 0), (0, 0), (0, ng8 - ng), (0, 0)))
    return ba[0], ba[1]


def kernel(x_prompt, x_sample, state_delta, state_conv, cache_k, cache_v, page_table, norm_ffn1, ffn1_w_gu, ffn1_w_dn, norm_mix, norm_ffn2, ffn2_w_gu, ffn2_w_dn, gdn_w_in, gdn_conv_w, gdn_a_log, gdn_dt_bias, gdn_norm_w, gdn_w_out, diff_w_in, diff_q_norm, diff_k_norm, diff_lam_q1, diff_lam_k1, diff_lam_q2, diff_lam_k2, diff_sub_norm, diff_w_out):
    bp, tp, d = x_prompt.shape
    bs, ts, _ = x_sample.shape
    depth = norm_ffn1.shape[0]
    n_pages = page_table.shape[1]
    past = n_pages * PAGE_SIZE
    C = GDN_CHUNK

    row = lambda w: w.reshape(w.shape[0], 1, w.shape[1])
    w_gu1, w_dn1, w_gu2, w_dn2 = ffn1_w_gu, ffn1_w_dn, ffn2_w_gu, ffn2_w_dn
    tn = 512
    tn_gdn = 1024
    gdn_n = GDN_CONV_DIM + GDN_VAL_DIM
    n_ba = 2 * GDN_V_HEADS
    w_gin_t = jnp.swapaxes(gdn_w_in, 1, 2)
    w_gout, w_din, w_dout = gdn_w_out, diff_w_in, diff_w_out
    g_ffn1, g_mix, g_ffn2 = row(norm_ffn1), row(norm_mix), row(norm_ffn2)
    alog_b, dtb_b = gdn_a_log, gdn_dt_bias
    gdn_nw = row(gdn_norm_w)
    qk_n = jnp.stack([diff_q_norm, diff_k_norm], axis=1).reshape(-1, 1, DIFF_HEAD)
    lq1, lk1, lq2, lk2 = row(diff_lam_q1), row(diff_lam_k1), row(diff_lam_q2), row(diff_lam_k2)
    sub_n = row(diff_sub_norm)
    ck = cache_k.reshape(cache_k.shape[0], cache_k.shape[1], PAGE_SIZE * DIFF_HEADS * 2, DIFF_HEAD)
    cv = cache_v.reshape(cache_v.shape[0], cache_v.shape[1], PAGE_SIZE * DIFF_HEADS, DIFF_V_HEAD)

    rope_p = _rope_tables(jnp.arange(tp))
    rope_p = tuple(jnp.tile(t, (bp, 1)) for t in rope_p)
    rope_s = _rope_tables(past + jnp.arange(ts))
    rope_s = tuple(jnp.tile(t, (bs, 1)) for t in rope_s)

    xp = x_prompt.reshape(bp * tp, d)
    xs = x_sample.reshape(bs * ts, d)
    mp, ms = bp * tp, bs * ts
    tm_p, tm_big, tm_s = 1024, 1024, ms
    tf, tf_cast = 512, 256
    ts_pad = -(-ts // C) * C

    prev8_p = jnp.zeros((bp, 8, GDN_CONV_DIM), F32)
    s0_p = jnp.zeros((bp, GDN_V_HEADS, GDN_HEAD, GDN_HEAD), F32)

    p_delta, p_conv = [], []
    n_attn = diff_w_in.shape[0]
    qkp = vp = None
    s_delta, s_conv, s_k, s_v = [], [], [], []
    for i in range(depth):
        xs, wg, wu, wd = _ffn_cast(xs, g_ffn1, w_gu1, w_dn1, i, tf_cast)
        xp = _ffn(xp, g_ffn1, wg, wu, wd, i, tm_p, tf)
        j = i // 2
        if i % 2 == 0:
            pp = _proj(xp, g_mix, w_gin_t, i, j, tm_big, tn_gdn, n_out=gdn_n, transposed=True, out_dtype=BF16)
            ps = _proj(xs, g_mix, w_gin_t, i, j, tm_s, tn_gdn, n_out=gdn_n, transposed=True)
            bap = _proj(xp, g_mix, w_gin_t, i, j, tm_big, n_ba, col0=gdn_n, n_out=n_ba, transposed=True)
            bas = _proj(xs, g_mix, w_gin_t, i, j, tm_s, n_ba, col0=gdn_n, n_out=n_ba, transposed=True)
            b_rows, a_rows = _gate_rows(bap, bp, tp)
            op, sp = _gdn_core(pp, prev8_p, gdn_conv_w, b_rows, a_rows, alog_b, dtb_b, gdn_nw, s0_p, j,
                               bp, tp, tp)
            p_conv.append(pp.reshape(bp, tp, gdn_n)[:, tp - (GDN_CONV - 1):, :GDN_CONV_DIM].astype(F32))
            p_delta.append(sp)
            xp = _outproj(op, w_gout, xp, j, tm_big, tn)

            ps3 = ps.reshape(bs, ts, gdn_n)
            ps_pad = jnp.pad(ps3, ((0, 0), (0, ts_pad - ts), (0, 0))).reshape(bs * ts_pad, gdn_n)
            prev8_s = jnp.pad(state_conv[j], ((0, 0), (8 - (GDN_CONV - 1), 0), (0, 0)))
            bas_pad = jnp.pad(bas.reshape(bs, ts, n_ba), ((0, 0), (0, ts_pad - ts), (0, 0)))
            b_rows, a_rows = _gate_rows(bas_pad.reshape(bs * ts_pad, n_ba), bs, ts_pad)
            os_, ss = _gdn_core(ps_pad, prev8_s, gdn_conv_w, b_rows, a_rows, alog_b, dtb_b, gdn_nw,
                                state_delta[j], j, bs, ts_pad, ts)
            xp_conv = jnp.concatenate([state_conv[j], ps3[:, :, :GDN_CONV_DIM]], axis=1)
            s_conv.append(xp_conv[:, ts:])
            s_delta.append(ss)
            os_ = os_.reshape(bs, ts_pad, GDN_VAL_DIM)[:, :ts].reshape(ms, GDN_VAL_DIM)
            xs = _outproj(os_, w_gout, xs, j, tm_s, tn)
        else:
            lam_init = 0.8 - 0.6 * math.exp(-0.3 * i)
            qkp = _qkproj(xp, g_mix, w_din, qk_n, *rope_p, i, j, tm_big, tn, prev=qkp)
            vp = _proj(xp, g_mix, w_din, i, j, tm_big, tn, col0=2 * DIFF_QK_DIM, n_out=DIFF_V_DIM,
                       planes=n_attn, prev=vp)
            qks = _qkproj(xs, g_mix, w_din, qk_n, *rope_s, i, j, tm_s, tn)
            vs = _proj(xs, g_mix, w_din, i, j, tm_s, tn, col0=2 * DIFF_QK_DIM, n_out=DIFF_V_DIM)
            qs, ks = qks[0], qks[1 + j]
            op = _flash(qkp, vp, lq1, lk1, lq2, lk2, sub_n, j, lam_init, bp, tp, 512)
            xp = _outproj(op, w_dout, xp, j, tm_big, tn)

            q_maps = qs.reshape(bs, ts, DIFF_HEADS, 2, DIFF_HEAD).transpose(0, 3, 2, 1, 4)
            q_maps = q_maps.reshape(bs, 2, DIFF_HEADS * ts, DIFF_HEAD)
            kn_pad = jnp.pad(ks.reshape(bs, ts * DIFF_HEADS * 2, DIFF_HEAD),
                             ((0, 0), (0, (PAGE_SIZE - ts) * DIFF_HEADS * 2), (0, 0)))
            vn_pad = jnp.pad(vs.reshape(bs, ts * DIFF_HEADS, DIFF_V_HEAD),
                             ((0, 0), (0, (PAGE_SIZE - ts) * DIFF_HEADS), (0, 0)))
            od = _decode(page_table, q_maps, kn_pad, vn_pad, ck, cv, lq1, lk1, lq2, lk2, sub_n,
                         j, lam_init, ts)
            od = od.reshape(bs, DIFF_HEADS, ts, DIFF_V_HEAD).transpose(0, 2, 1, 3).reshape(ms, DIFF_V_DIM)
            xs = _outproj(od.astype(BF16), w_dout, xs, j, tm_s, tn)
            s_k.append(ks.reshape(bs, ts, DIFF_HEADS, 2, DIFF_HEAD))
            s_v.append(vs.reshape(bs, ts, DIFF_HEADS, DIFF_V_HEAD))
        xs, wg, wu, wd = _ffn_cast(xs, g_ffn2, w_gu2, w_dn2, i, tf_cast)
        xp = _ffn(xp, g_ffn2, wg, wu, wd, i, tm_p, tf)

    p_k = qkp[1:].reshape(n_attn, bp, tp, DIFF_HEADS, 2, DIFF_HEAD)
    p_v = vp.reshape(n_attn, bp, tp, DIFF_HEADS, DIFF_V_HEAD)
    return (xp.reshape(bp, tp, d), xs.reshape(bs, ts, d),
            jnp.stack(p_delta), jnp.stack(p_conv), p_k, p_v,
            jnp.stack(s_delta), jnp.stack(s_conv), jnp.stack(s_k), jnp.stack(s_v))
```
